```python
import jax
import jax.numpy as jnp
from jax import lax
import numpy as np

D_MODEL = 2048
BATCH = 4
SEQ = 2048
DEPTH = 1

CTX_LEN = 256
GRID_W = 64
EPS = 1e-6
N_MOD = 6

CONV_DIM = 2048
CONV_WIDTH = 31

DN_HEADS = 16
DN_DK = 128
DN_DV = 128
DN_QK = DN_HEADS * DN_DK
DN_V = DN_HEADS * DN_DV
SHORT_CONV = 5
CHUNK = 64

PEER_HEADS = 8
PEER_KEYS = 128
PEER_EXPERTS = PEER_KEYS * PEER_KEYS
PEER_QDIM = 256
PEER_HALF = PEER_QDIM // 2
PEER_TOPK = 16
PEER_BLOCK = 128

COL_GLU = 2 * CONV_DIM
COL_QKV = 2 * DN_QK + DN_V
COL_Z = DN_V
COL_AB = 4 * DN_HEADS
COL_BR = 2 * D_MODEL
IN_COLS = COL_GLU + COL_QKV + COL_Z + COL_AB + COL_BR

kernel_name = 'hybrid_conv_gdn_peer_prefix_dit'


def rms_norm(x, g):
    xf = x.astype(jnp.float32)
    y = xf * lax.rsqrt(jnp.mean(xf * xf, axis=-1, keepdims=True) + EPS)
    return (y * g.astype(jnp.float32)).astype(x.dtype)


def layer_norm(x, g, b):
    xf = x.astype(jnp.float32)
    mu = jnp.mean(xf, axis=-1, keepdims=True)
    var = jnp.mean(jnp.square(xf - mu), axis=-1, keepdims=True)
    y = (xf - mu) * lax.rsqrt(var + EPS)
    return (y * g.astype(jnp.float32) + b.astype(jnp.float32)).astype(x.dtype)


def l2norm(x):
    xf = x.astype(jnp.float32)
    return xf * lax.rsqrt(jnp.sum(xf * xf, axis=-1, keepdims=True) + EPS)


def modulate(h, shift, scale):
    return h * (1.0 + scale[:, None, :]) + shift[:, None, :]


def dw_conv(x, w):
    pad = w.shape[0] // 2
    return lax.conv_general_dilated(x, w[:, None, :].astype(x.dtype), (1,), [(pad, pad)],
                                    dimension_numbers=('NWC', 'WIO', 'NWC'),
                                    feature_group_count=x.shape[-1])


def split_in_proj(p):
    s1 = COL_GLU
    s2 = s1 + COL_QKV
    s3 = s2 + COL_Z
    s4 = s3 + COL_AB
    return jnp.split(p, [s1, s2, s3, s4], axis=-1)


def conformer_conv(glu_in, conv_w, conv_b, ln_g, ln_b, w_conv_out, rows):
    a, gate = jnp.split(glu_in, 2, axis=-1)
    u = a * jax.nn.sigmoid(gate)
    B, L, C = u.shape
    if rows is not None:
        u = u.reshape(B * rows, GRID_W, C)
    u = (dw_conv(u, conv_w) + conv_b).reshape(B, L, C)
    u = jax.nn.silu(layer_norm(u, ln_g, ln_b))
    return u @ w_conv_out


def dn_qkv(qkv, short_w):
    B, L, _ = qkv.shape
    qkv = jax.nn.silu(dw_conv(qkv, short_w))
    q, k, v = jnp.split(qkv, [DN_QK, 2 * DN_QK], axis=-1)
    q = l2norm(q.reshape(B, L, DN_HEADS, DN_DK)) * (DN_DK ** -0.5)
    k = l2norm(k.reshape(B, L, DN_HEADS, DN_DK))
    v = v.reshape(B, L, DN_HEADS, DN_DV)
    return q, k, v


def dn_gates(ab, a_log, dt_bias):
    a, b = jnp.split(ab.astype(jnp.float32), 2, axis=-1)
    g = -jnp.exp(a_log.astype(jnp.float32)) * jax.nn.softplus(a + dt_bias.astype(jnp.float32))
    beta = jax.nn.sigmoid(b)
    return g, beta


def gated_delta_chunked(q, k, v, g, beta, s0):
    B, L, H, DK = q.shape
    DV = v.shape[-1]
    n = L // CHUNK

    def chunks(t):
        t = t.astype(jnp.float32).reshape((B, n, CHUNK) + t.shape[2:])
        return jnp.moveaxis(t, 1, 0).swapaxes(2, 3)

    qc, kc, vc, gc, bc = chunks(q), chunks(k), chunks(v), chunks(g), chunks(beta)
    gcum = jnp.cumsum(gc, axis=-1)
    tri = jnp.tril(jnp.ones((CHUNK, CHUNK), dtype=bool))
    strict = jnp.tril(jnp.ones((CHUNK, CHUNK), dtype=bool), -1)
    decay = jnp.exp(jnp.where(tri, gcum[..., :, None] - gcum[..., None, :], -jnp.inf))
    kb = kc * bc[..., None]
    m = jnp.where(strict, jnp.einsum('nbhik,nbhjk->nbhij', kb, kc) * decay, 0.0)
    eye = jnp.eye(CHUNK, dtype=jnp.float32)
    t_inv = lax.linalg.triangular_solve(m + eye, jnp.broadcast_to(eye, m.shape), left_side=True,
                                        lower=True, unit_diagonal=True)
    u = jnp.einsum('nbhij,nbhjv->nbhiv', t_inv, vc * bc[..., None])
    w = jnp.einsum('nbhij,nbhjk->nbhik', t_inv, kb * jnp.exp(gcum)[..., None])
    attn = jnp.einsum('nbhik,nbhjk->nbhij', qc, kc) * decay

    def step(S, inp):
        q_i, k_i, u_i, w_i, g_i, a_i = inp
        v_new = u_i - jnp.einsum('bhck,bhkv->bhcv', w_i, S)
        o = (jnp.einsum('bhck,bhkv->bhcv', q_i * jnp.exp(g_i)[..., None], S)
             + jnp.einsum('bhij,bhjv->bhiv', a_i, v_new))
        g_last = g_i[..., -1:]
        S = (S * jnp.exp(g_last)[..., None]
             + jnp.einsum('bhck,bhcv->bhkv', k_i * jnp.exp(g_last - g_i)[..., None], v_new))
        return S, o

    s_final, o = lax.scan(step, s0.astype(jnp.float32), (qc, kc, u, w, gcum, attn))
    o = jnp.moveaxis(o.swapaxes(2, 3), 0, 1).reshape(B, L, H, DV)
    return s_final, o


def bidirectional_deltanet(qkv_l, qkv_c, ab_l, ab_c, short_w, a_log, dt_bias):
    ql, kl, vl = dn_qkv(qkv_l, short_w)
    qc, kc, vc = dn_qkv(qkv_c, short_w)
    B = ql.shape[0]
    s0 = jnp.zeros((B, DN_HEADS, DN_DK, DN_DV), jnp.float32)
    o_l = jnp.zeros(vl.shape, jnp.float32)
    o_c = jnp.zeros(vc.shape, jnp.float32)
    for d in range(2):
        cols = slice(2 * d * DN_HEADS, (2 * d + 2) * DN_HEADS)
        gl, bl = dn_gates(ab_l[..., cols], a_log[d], dt_bias[d])
        gc, bc = dn_gates(ab_c[..., cols], a_log[d], dt_bias[d])
        seq_l = [ql, kl, vl, gl, bl]
        seq_c = [qc, kc, vc, gc, bc]
        if d == 1:
            seq_l = [jnp.flip(t, axis=1) for t in seq_l]
            seq_c = [jnp.flip(t, axis=1) for t in seq_c]
        s_ctx, oc = gated_delta_chunked(*seq_c, s0)
        _, ol = gated_delta_chunked(*seq_l, s_ctx)
        if d == 1:
            oc = jnp.flip(oc, axis=1)
            ol = jnp.flip(ol, axis=1)
        o_l = o_l + ol
        o_c = o_c + oc
    return o_l, o_c


def token_mixer(h, hc, w_in, conv_w, conv_b, conv_ln_g, conv_ln_b, w_conv_out, dn_short_w,
                dn_a_log, dn_dt_bias, dn_norm_g, w_dn_out, w_out, with_ctx_out):
    B, L, _ = h.shape
    rows = L // GRID_W
    glu_l, qkv_l, z_l, ab_l, br_l = split_in_proj(h @ w_in)
    glu_c, qkv_c, z_c, ab_c, br_c = split_in_proj(hc @ w_in)
    o_l, o_c = bidirectional_deltanet(qkv_l, qkv_c, ab_l, ab_c, dn_short_w, dn_a_log, dn_dt_bias)

    def merge(glu, z, br, o, rows_):
        Bq, Lq, _ = glu.shape
        y_conv = conformer_conv(glu, conv_w, conv_b, conv_ln_g, conv_ln_b, w_conv_out, rows_)
        o = rms_norm(o, dn_norm_g).astype(glu.dtype) * jax.nn.silu(z.reshape(Bq, Lq, DN_HEADS, DN_DV))
        y_dn = o.reshape(Bq, Lq, DN_V) @ w_dn_out
        g_conv, g_dn = jnp.split(jax.nn.sigmoid(br), 2, axis=-1)
        return (g_conv * y_conv + g_dn * y_dn) @ w_out

    y = merge(glu_l, z_l, br_l, o_l, rows)
    yc = merge(glu_c, z_c, br_c, o_c, None) if with_ctx_out else None
    return y, yc


def peer(h, w_q, key1, key2, u_tab, v_tab):
    B, L, D = h.shape
    T = B * L
    t = h.reshape(T, D)
    q = (t @ w_q).reshape(T, PEER_HEADS, PEER_QDIM).astype(jnp.float32)
    s1 = jnp.einsum('thd,hkd->thk', q[..., :PEER_HALF], key1.astype(jnp.float32))
    s2 = jnp.einsum('thd,hkd->thk', q[..., PEER_HALF:], key2.astype(jnp.float32))
    v1, i1 = lax.top_k(s1, PEER_TOPK)
    v2, i2 = lax.top_k(s2, PEER_TOPK)
    cand = (v1[..., :, None] + v2[..., None, :]).reshape(T, PEER_HEADS, PEER_TOPK * PEER_TOPK)
    cidx = (i1[..., :, None] * PEER_KEYS + i2[..., None, :]).reshape(T, PEER_HEADS, PEER_TOPK * PEER_TOPK)
    best, pos = lax.top_k(cand, PEER_TOPK)
    idx = jnp.take_along_axis(cidx, pos, axis=-1)
    wgt = jax.nn.softmax(best, axis=-1).astype(h.dtype)
    nb = T // PEER_BLOCK

    def expert_block(args):
        tb, ib, wb = args
        act = jax.nn.gelu(jnp.einsum('td,thkd->thk', tb, u_tab[ib]))
        return jnp.einsum('thk,thkd->td', wb * act, v_tab[ib])

    y = lax.map(expert_block, (t.reshape(nb, PEER_BLOCK, D),
                               idx.reshape(nb, PEER_BLOCK, PEER_HEADS, PEER_TOPK),
                               wgt.reshape(nb, PEER_BLOCK, PEER_HEADS, PEER_TOPK)))
    return y.reshape(B, L, D)


def setup_inputs(seed: int = 0) -> dict:
    key = jax.random.key(seed)
    ks = jax.random.split(key, 28)
    f32 = jnp.float32
    D = D_MODEL

    def nrm(k, shape, scale):
        return jax.random.normal(k, shape, f32) * scale

    dt = jnp.exp(jax.random.uniform(ks[14], (DEPTH, 2, DN_HEADS), f32,
                                    float(np.log(1e-3)), float(np.log(1e-1))))
    return {
        'x': nrm(ks[0], (BATCH, SEQ, D), 1.0),
        'c': nrm(ks[1], (BATCH, D), 1.0),
        'ctx': nrm(ks[2], (BATCH, CTX_LEN, D), 1.0),
        'c_ctx': nrm(ks[3], (D,), 1.0),
        'w_mod': nrm(ks[4], (DEPTH, D, N_MOD * D), 0.5 * D ** -0.5),
        'b_mod': nrm(ks[5], (DEPTH, N_MOD * D), 0.02),
        'norm1_g': 1.0 + nrm(ks[6], (DEPTH, D), 0.05),
        'norm2_g': 1.0 + nrm(ks[7], (DEPTH, D), 0.05),
        'w_in': nrm(ks[8], (DEPTH, D, IN_COLS), D ** -0.5),
        'conv_w': nrm(ks[9], (DEPTH, CONV_WIDTH, CONV_DIM), CONV_WIDTH ** -0.5),
        'conv_b': nrm(ks[10], (DEPTH, CONV_DIM), 0.02),
        'conv_ln_g': 1.0 + nrm(ks[11], (DEPTH, CONV_DIM), 0.05),
        'conv_ln_b': nrm(ks[12], (DEPTH, CONV_DIM), 0.02),
        'w_conv_out': nrm(ks[13], (DEPTH, CONV_DIM, D), CONV_DIM ** -0.5),
        'dn_short_w': nrm(ks[15], (DEPTH, SHORT_CONV, COL_QKV), SHORT_CONV ** -0.5),
        'dn_a_log': jnp.log(jax.random.uniform(ks[16], (DEPTH, 2, DN_HEADS), f32, 1.0, 16.0)),
        'dn_dt_bias': dt + jnp.log(-jnp.expm1(-dt)),
        'dn_norm_g': 1.0 + nrm(ks[17], (DEPTH, DN_DV), 0.05),
        'w_dn_out': nrm(ks[18], (DEPTH, DN_V, D), DN_V ** -0.5),
        'w_out': nrm(ks[19], (DEPTH, D, D), D ** -0.5),
        'peer_w_q': nrm(ks[20], (DEPTH, D, PEER_HEADS * PEER_QDIM), D ** -0.5),
        'peer_key1': nrm(ks[21], (DEPTH, PEER_HEADS, PEER_KEYS, PEER_HALF), PEER_HALF ** -0.5),
        'peer_key2': nrm(ks[22], (DEPTH, PEER_HEADS, PEER_KEYS, PEER_HALF), PEER_HALF ** -0.5),
        'peer_u': nrm(ks[23], (DEPTH, PEER_EXPERTS, D), D ** -0.5),
        'peer_v': nrm(ks[24], (DEPTH, PEER_EXPERTS, D), PEER_HEADS ** -0.5),
        'final_g': 1.0 + nrm(ks[25], (D,), 0.05),
    }


def reference(x, c, ctx, c_ctx, w_mod, b_mod, norm1_g, norm2_g, w_in, conv_w, conv_b, conv_ln_g,
              conv_ln_b, w_conv_out, dn_short_w, dn_a_log, dn_dt_bias, dn_norm_g, w_dn_out, w_out,
              peer_w_q, peer_key1, peer_key2, peer_u, peer_v, final_g):
    xc = ctx
    sc = jax.nn.silu(c)
    scc = jax.nn.silu(c_ctx)[None, :]
    for l in range(DEPTH):
        last = l == DEPTH - 1
        mod = jnp.split(sc @ w_mod[l] + b_mod[l], N_MOD, axis=-1)
        modc = jnp.split(scc @ w_mod[l] + b_mod[l], N_MOD, axis=-1)
        h = modulate(rms_norm(x, norm1_g[l]), mod[0], mod[1])
        hc = modulate(rms_norm(xc, norm1_g[l]), modc[0], modc[1])
        y, yc = token_mixer(h, hc, w_in[l], conv_w[l], conv_b[l], conv_ln_g[l], conv_ln_b[l],
                            w_conv_out[l], dn_short_w[l], dn_a_log[l], dn_dt_bias[l], dn_norm_g[l],
                            w_dn_out[l], w_out[l], not last)
        x = x + mod[2][:, None, :] * y
        h = modulate(rms_norm(x, norm2_g[l]), mod[3], mod[4])
        x = x + mod[5][:, None, :] * peer(h, peer_w_q[l], peer_key1[l], peer_key2[l], peer_u[l], peer_v[l])
        if not last:
            xc = xc + modc[2][:, None, :] * yc
            hc = modulate(rms_norm(xc, norm2_g[l]), modc[3], modc[4])
            xc = xc + modc[5][:, None, :] * peer(hc, peer_w_q[l], peer_key1[l], peer_key2[l],
                                                 peer_u[l], peer_v[l])
    return rms_norm(x, final_g)
```

```python
import functools

import jax
import jax.numpy as jnp
from jax import lax
from jax.experimental import pallas as pl
from jax.experimental.pallas import tpu as pltpu

F32 = jnp.float32
BF16 = jnp.bfloat16
HI = lax.Precision.HIGHEST

EPS = 1e-6
N_MOD = 6
GRID_W = 64
CONV_WIDTH = 31
DN_HEADS = 16
DN_DK = 128
SHORT_CONV = 5
PEER_HEADS = 8
PEER_KEYS = 128
PEER_TOPK = 16

LANES = 128
DN_CHUNK = 128
NEG_BIG = -3.0e38
VMEM_LIMIT = 56 << 20


def _params(sem):
    return pltpu.CompilerParams(dimension_semantics=sem, vmem_limit_bytes=VMEM_LIMIT)


def _sigmoid(x):
    return 1.0 / (1.0 + jnp.exp(-x))


def _silu(x):
    return x * _sigmoid(x)


def _dot(a, b, precision=None):
    return jnp.dot(a, b, preferred_element_type=F32, precision=precision)


def _dot_nt(a, b, precision=None):
    return lax.dot_general(a, b, (((1,), (1,)), ((), ())), preferred_element_type=F32,
                           precision=precision)


def _split_bf16(a):
    hi = a.astype(BF16)
    lo = (a - hi.astype(F32)).astype(BF16)
    return hi, lo


def _dot3(a, b):
    ah, al = _split_bf16(a)
    bh, bl = _split_bf16(b)
    return _dot(ah, bh) + (_dot(ah, bl) + _dot(al, bh))


def _dot3_nt(a, b):
    ah, al = _split_bf16(a)
    bh, bl = _split_bf16(b)
    return _dot_nt(ah, bh) + (_dot_nt(ah, bl) + _dot_nt(al, bh))


def _mod_kernel(c_ref, w_ref, b_ref, o_ref):
    c = c_ref[...]
    o_ref[...] = _dot(_silu(c), w_ref[...], HI) + b_ref[...]


def _mod(c8, w_mod, b_mod):
    d, n = w_mod.shape
    tn = 1024
    return pl.pallas_call(
        _mod_kernel,
        grid=(n // tn,),
        in_specs=[pl.BlockSpec((8, d), lambda j: (0, 0)),
                  pl.BlockSpec((d, tn), lambda j: (0, j)),
                  pl.BlockSpec((1, tn), lambda j: (0, j))],
        out_specs=pl.BlockSpec((8, tn), lambda j: (0, j)),
        out_shape=jax.ShapeDtypeStruct((8, n), F32),
        compiler_params=_params(("arbitrary",)),
        name="mod",
    )(c8, w_mod, b_mod.reshape(1, n))


def _norm_proj_kernel(x_ref, g_ref, sh_ref, sc_ref, w_ref, *rest, with_ab, with_h):
    rest = list(rest)
    wab_ref = rest.pop(0) if with_ab else None
    o_ref = rest.pop(0)
    oab_ref = rest.pop(0) if with_ab else None
    oh_ref = rest.pop(0) if with_h else None
    h_ref = rest.pop(0)

    @pl.when(pl.program_id(1) == 0)
    def _():
        x = x_ref[...]
        ms = jnp.mean(x * x, axis=-1, keepdims=True)
        y = x * lax.rsqrt(ms + EPS) * g_ref[...]
        h = y * (1.0 + sc_ref[0]) + sh_ref[0]
        hb = h.astype(BF16)
        h_ref[...] = hb
        if with_ab:
            oab_ref[...] = _dot(h, wab_ref[...], HI)
        if with_h:
            oh_ref[...] = hb

    o_ref[...] = _dot(h_ref[...], w_ref[...])


def _norm_proj(x2d, g, mods, shift_chunk, scale_chunk, rows_per_mod, mod_row0, w_bf16, w_ab=None,
               with_h=False, tn=512, name="norm_proj"):
    rows, d = x2d.shape
    n = w_bf16.shape[1]
    with_ab = w_ab is not None
    tm = min(1024, rows_per_mod)
    assert rows % tm == 0 and rows_per_mod % tm == 0 and n % tn == 0

    def batch_of_tile(i):
        return mod_row0 + (i * tm) // rows_per_mod

    in_specs = [pl.BlockSpec((tm, d), lambda i, j: (i, 0)),
                pl.BlockSpec((1, d), lambda i, j: (0, 0)),
                pl.BlockSpec((1, 1, d), lambda i, j: (batch_of_tile(i), 0, shift_chunk)),
                pl.BlockSpec((1, 1, d), lambda i, j: (batch_of_tile(i), 0, scale_chunk)),
                pl.BlockSpec((d, tn), lambda i, j: (0, j))]
    args = [x2d, g.reshape(1, d), mods, mods, w_bf16]
    out_specs = [pl.BlockSpec((tm, tn), lambda i, j: (i, j))]
    out_shape = [jax.ShapeDtypeStruct((rows, n), F32)]
    if with_ab:
        in_specs.append(pl.BlockSpec((d, LANES), lambda i, j: (0, 0)))
        args.append(w_ab)
        out_specs.append(pl.BlockSpec((tm, LANES), lambda i, j: (i, 0)))
        out_shape.append(jax.ShapeDtypeStruct((rows, LANES), F32))
    if with_h:
        out_specs.append(pl.BlockSpec((tm, d), lambda i, j: (i, 0)))
        out_shape.append(jax.ShapeDtypeStruct((rows, d), BF16))
    return pl.pallas_call(
        functools.partial(_norm_proj_kernel, with_ab=with_ab, with_h=with_h),
        grid=(rows // tm, n // tn),
        in_specs=in_specs,
        out_specs=out_specs,
        out_shape=out_shape,
        scratch_shapes=[pltpu.VMEM((tm, d), BF16)],
        compiler_params=_params(("arbitrary", "arbitrary")),
        name=name,
    )(*args)


CONV_ROWS = 4
CONV_CW = 256
CONV_PAD = 16


def _conv_kernel(a_ref, gt_ref, cw_ref, cb_ref, lg_ref, lb_ref, w_ref, o_ref, pad_ref, y_ref):
    c = a_ref.shape[1]
    ncc = c // CONV_CW
    u = a_ref[...] * _sigmoid(gt_ref[...])
    zero = jnp.zeros((CONV_PAD, CONV_CW), F32)
    for r in range(CONV_ROWS):
        for cc in range(ncc):
            pad_ref[r, cc, 0:CONV_PAD, :] = zero
            pad_ref[r, cc, CONV_PAD:CONV_PAD + GRID_W, :] = (
                u[r * GRID_W:(r + 1) * GRID_W, cc * CONV_CW:(cc + 1) * CONV_CW])
            pad_ref[r, cc, CONV_PAD + GRID_W:2 * CONV_PAD + GRID_W, :] = zero

    half = CONV_WIDTH // 2

    def body(idx, carry):
        r = idx // ncc
        cc = idx % ncc
        acc = jnp.zeros((GRID_W, CONV_CW), F32)
        for k in range(CONV_WIDTH):
            start = CONV_PAD - half + k
            acc = acc + pad_ref[r, cc, start:start + GRID_W, :] * cw_ref[cc, k:k + 1, :]
        y_ref[cc, pl.ds(pl.multiple_of(r * GRID_W, GRID_W), GRID_W), :] = acc
        return carry

    lax.fori_loop(0, CONV_ROWS * ncc, body, 0)

    y = jnp.concatenate([y_ref[cc] for cc in range(ncc)], axis=-1) + cb_ref[...]
    mu = jnp.mean(y, axis=-1, keepdims=True)
    yc = y - mu
    var = jnp.mean(yc * yc, axis=-1, keepdims=True)
    yn = yc * lax.rsqrt(var + EPS) * lg_ref[...] + lb_ref[...]
    o_ref[...] = _dot(_silu(yn).astype(BF16), w_ref[...])


def _conformer(p, conv_w, conv_b, ln_g, ln_b, w_out_bf16, c):
    rows = p.shape[0]
    tm = CONV_ROWS * GRID_W
    ncc = c // CONV_CW
    cw = jnp.pad(conv_w, ((0, 32 - CONV_WIDTH), (0, 0)))
    cw = cw.reshape(32, ncc, CONV_CW).transpose(1, 0, 2)
    return pl.pallas_call(
        _conv_kernel,
        grid=(rows // tm,),
        in_specs=[pl.BlockSpec((tm, c), lambda i: (i, 0)),
                  pl.BlockSpec((tm, c), lambda i: (i, 1)),
                  pl.BlockSpec((ncc, 32, CONV_CW), lambda i: (0, 0, 0)),
                  pl.BlockSpec((1, c), lambda i: (0, 0)),
                  pl.BlockSpec((1, c), lambda i: (0, 0)),
                  pl.BlockSpec((1, c), lambda i: (0, 0)),
                  pl.BlockSpec((c, c), lambda i: (0, 0))],
        out_specs=pl.BlockSpec((tm, c), lambda i: (i, 0)),
        out_shape=jax.ShapeDtypeStruct((rows, c), F32),
        scratch_shapes=[pltpu.VMEM((CONV_ROWS, ncc, GRID_W + 2 * CONV_PAD, CONV_CW), F32),
                        pltpu.VMEM((ncc, tm, CONV_CW), F32)],
        compiler_params=_params(("arbitrary",)),
        name="conformer",
    )(p, p, cw, conv_b.reshape(1, c), ln_g.reshape(1, c), ln_b.reshape(1, c), w_out_bf16)


DN_PAD = 8
DN_RC = 256


def _dnprep_kernel(q_ref, k_ref, v_ref, wq_ref, wk_ref, wv_ref, qo_ref, ko_ref, vo_ref, pad_ref):
    length = q_ref.shape[0]
    zero = jnp.zeros((DN_PAD, LANES), F32)
    half = SHORT_CONV // 2

    def run(x_ref, w_ref, o_ref, mode):
        pad_ref[0:DN_PAD, :] = zero
        pad_ref[DN_PAD + length:2 * DN_PAD + length, :] = zero
        pad_ref[DN_PAD:DN_PAD + length, :] = x_ref[...]
        for r0 in range(0, length, DN_RC):
            acc = jnp.zeros((DN_RC, LANES), F32)
            for k in range(SHORT_CONV):
                start = DN_PAD - half + k + r0
                acc = acc + pad_ref[start:start + DN_RC, :] * w_ref[k:k + 1, :]
            y = _silu(acc)
            if mode != "v":
                y = y * lax.rsqrt(jnp.sum(y * y, axis=-1, keepdims=True) + EPS)
            if mode == "q":
                y = y * (DN_DK ** -0.5)
            o_ref[0, r0:r0 + DN_RC, :] = y

    run(q_ref, wq_ref, qo_ref, "q")
    run(k_ref, wk_ref, ko_ref, "k")
    run(v_ref, wv_ref, vo_ref, "v")


def _dn_prep(p, col0, short_w8, batch, length):
    nh = DN_HEADS
    out = jax.ShapeDtypeStruct((batch, length, nh * LANES), F32)
    ospec = pl.BlockSpec((1, length, LANES), lambda b, h: (b, 0, h))
    return pl.pallas_call(
        _dnprep_kernel,
        grid=(batch, nh),
        in_specs=[pl.BlockSpec((length, LANES), lambda b, h: (b, col0 + h)),
                  pl.BlockSpec((length, LANES), lambda b, h: (b, col0 + nh + h)),
                  pl.BlockSpec((length, LANES), lambda b, h: (b, col0 + 2 * nh + h)),
                  pl.BlockSpec((8, LANES), lambda b, h: (0, h)),
                  pl.BlockSpec((8, LANES), lambda b, h: (0, nh + h)),
                  pl.BlockSpec((8, LANES), lambda b, h: (0, 2 * nh + h))],
        out_specs=[ospec, ospec, ospec],
        out_shape=[out, out, out],
        scratch_shapes=[pltpu.VMEM((length + 2 * DN_PAD, LANES), F32)],
        compiler_params=_params(("arbitrary", "arbitrary")),
        name="dn_prep",
    )(p, p, p, short_w8, short_w8, short_w8)


DN_HB = 4


def _scan_kernel(qc_ref, kc_ref, vc_ref, abc_ref, ql_ref, kl_ref, vl_ref, abl_ref,
                 alog_ref, dtb_ref, o_ref, s_ref):
    ch = DN_CHUNK
    hb = DN_HB
    hg = pl.program_id(1)
    d = pl.program_id(2)
    fwd = d == 0

    row = lax.broadcasted_iota(jnp.int32, (ch, ch), 0)
    col = lax.broadcasted_iota(jnp.int32, (ch, ch), 1)
    diff = jnp.where(fwd, row - col, col - row)
    incl = diff >= 0
    strict = diff > 0
    tri = incl.astype(F32)
    eye = (row == col).astype(F32)
    ones = jnp.ones((ch, ch), F32)
    base = 2 * DN_HEADS * d + hg * hb
    sel_a = ((row - base == col) & (col < hb)).astype(F32)
    sel_b = ((row - base - DN_HEADS == col) & (col < hb)).astype(F32)
    row_e = lax.broadcasted_iota(jnp.int32, (ch, hb * ch), 0)
    col_e = lax.broadcasted_iota(jnp.int32, (ch, hb * ch), 1)
    expand = ((col_e // ch) == row_e).astype(F32)
    diag = ((col_e % ch) == row_e).astype(F32)

    neg_a = -jnp.exp(alog_ref[...])
    dtb = dtb_ref[...]

    s_ref[...] = jnp.zeros_like(s_ref)

    @pl.when(fwd)
    def _():
        o_ref[...] = jnp.zeros_like(o_ref)

    def neumann(m):
        x = -m
        t = eye + x
        steps = ch.bit_length() - 2
        for _ in range(steps):
            x = _dot3(x, x)
            t = t + _dot3(t, x)
        return t

    def phase(q_ref, k_ref, v_ref, ab_ref, n_chunks, write_out):
        def body(p, carry):
            ci = jnp.where(fwd, p, n_chunks - 1 - p)
            r0 = pl.multiple_of(ci * ch, ch)
            ab = ab_ref[0, pl.ds(r0, ch), :]
            x = ab + dtb
            softplus = jnp.maximum(x, 0.0) + jnp.log(1.0 + jnp.exp(-jnp.abs(x)))
            g_sel = _dot(neg_a * softplus, sel_a, HI)
            b_sel = _dot(_sigmoid(ab), sel_b, HI)
            g_cum = _dot(tri, g_sel, HI)
            g_tot = _dot(ones, g_sel, HI)
            gcol = _dot(g_cum, expand, HI)
            bcol = _dot(b_sel, expand, HI)
            gtot = _dot(g_tot, expand, HI)
            grow = _dot(ones, gcol * diag, HI)
            gdiff = gcol - grow
            for c in range(hb):
                sl = slice(c * ch, (c + 1) * ch)
                q = q_ref[0, pl.ds(r0, ch), sl]
                k = k_ref[0, pl.ds(r0, ch), sl]
                v = v_ref[0, pl.ds(r0, ch), sl]
                beta = bcol[:, sl]
                gc = gcol[:, sl]
                gt = gtot[:, sl]
                eg = jnp.exp(gc)
                decay = jnp.where(incl, jnp.exp(gdiff[:, sl]), 0.0)
                kb = k * beta
                m = jnp.where(strict, _dot3_nt(kb, k) * decay, 0.0)
                t_inv = neumann(m)
                u = _dot3(t_inv, v * beta)
                w = _dot3(t_inv, kb * eg)
                attn = _dot_nt(q, k) * decay
                s = s_ref[c]
                v_new = u - _dot(w, s)
                if write_out:
                    o = _dot(q * eg, s) + _dot(attn, v_new)
                    o_ref[0, pl.ds(r0, ch), sl] += o
                kd = k * jnp.exp(gt - gc)
                s_ref[c] = s * jnp.exp(gt) + _dot(kd.T, v_new)
            return carry

        lax.fori_loop(0, n_chunks, body, 0)

    phase(qc_ref, kc_ref, vc_ref, abc_ref, qc_ref.shape[1] // ch, False)
    phase(ql_ref, kl_ref, vl_ref, abl_ref, ql_ref.shape[1] // ch, True)


def _dn_scan(qc, kc, vc, abc, ql, kl, vl, abl, alog_l, dtb_l):
    batch, lc, hv = qc.shape
    ll = ql.shape[1]
    hb = DN_HB
    w = hb * LANES
    cspec = pl.BlockSpec((1, lc, w), lambda b, g, d: (b, 0, g))
    lspec = pl.BlockSpec((1, ll, w), lambda b, g, d: (b, 0, g))
    return pl.pallas_call(
        _scan_kernel,
        grid=(batch, DN_HEADS // hb, 2),
        in_specs=[cspec, cspec, cspec,
                  pl.BlockSpec((1, lc, LANES), lambda b, g, d: (b, 0, 0)),
                  lspec, lspec, lspec,
                  pl.BlockSpec((1, ll, LANES), lambda b, g, d: (b, 0, 0)),
                  pl.BlockSpec((1, LANES), lambda b, g, d: (0, 0)),
                  pl.BlockSpec((1, LANES), lambda b, g, d: (0, 0))],
        out_specs=lspec,
        out_shape=jax.ShapeDtypeStruct((batch, ll, hv), F32),
        scratch_shapes=[pltpu.VMEM((hb, DN_DK, LANES), F32)],
        compiler_params=_params(("arbitrary", "arbitrary", "arbitrary")),
        name="dn_scan",
    )(qc, kc, vc, abc, ql, kl, vl, abl, alog_l, dtb_l)


def _merge_kernel(yc_ref, o_ref, z_ref, brc_ref, brd_ref, g_ref, wdn_ref, wout_ref, x_ref,
                  m2_ref, out_ref):
    o = o_ref[...]
    parts = []
    for h in range(DN_HEADS):
        oh = o[:, h * LANES:(h + 1) * LANES]
        ms = jnp.mean(oh * oh, axis=-1, keepdims=True)
        parts.append(oh * lax.rsqrt(ms + EPS))
    on = jnp.concatenate(parts, axis=-1) * g_ref[...]
    gated = on * _silu(z_ref[...])
    y_dn = _dot(gated.astype(BF16), wdn_ref[...])
    mix = _sigmoid(brc_ref[...]) * yc_ref[...] + _sigmoid(brd_ref[...]) * y_dn
    y = _dot(mix.astype(BF16), wout_ref[...])
    out_ref[...] = x_ref[...] + m2_ref[0] * y


def _merge(y_conv, o2d, p, zcol, brcol, norm_g_l, wdn, wout, x2d, mods, seq, tm=256):
    rows, d = x2d.shape
    rspec = lambda cb: pl.BlockSpec((tm, d), lambda i: (i, cb))
    wspec = pl.BlockSpec((d, d), lambda i: (0, 0), pipeline_mode=pl.Buffered(1))
    return pl.pallas_call(
        _merge_kernel,
        grid=(rows // tm,),
        in_specs=[rspec(0), rspec(0), rspec(zcol), rspec(brcol), rspec(brcol + 1),
                  pl.BlockSpec((1, d), lambda i: (0, 0)), wspec, wspec, rspec(0),
                  pl.BlockSpec((1, 1, d), lambda i: ((i * tm) // seq, 0, 2))],
        out_specs=rspec(0),
        out_shape=jax.ShapeDtypeStruct((rows, d), F32),
        compiler_params=_params(("arbitrary",)),
        name="merge",
    )(y_conv, o2d, p, p, p, norm_g_l, wdn, wout, x2d, mods)


ROUTE_TM = 256


def _top_values(vals, count):
    outs = []
    for _ in range(count):
        m = jnp.max(vals, axis=0, keepdims=True)
        outs.append(m)
        vals = jnp.where(vals == m, NEG_BIG, vals)
    return outs


def _route_kernel(q_ref, k1_ref, k2_ref, s1_ref, s2_ref, p1_ref, p2_ref, tau_ref):
    q = q_ref[...]
    half = PEER_KEYS
    for h in range(PEER_HEADS):
        q1 = q[:, (2 * h) * half:(2 * h + 1) * half]
        q2 = q[:, (2 * h + 1) * half:(2 * h + 2) * half]
        s1 = _dot_nt(k1_ref[h], q1, HI)
        s2 = _dot_nt(k2_ref[h], q2, HI)
        v1 = _top_values(s1, PEER_TOPK)
        v2 = _top_values(s2, PEER_TOPK)
        v2s = jnp.concatenate(v2, axis=0)
        cand = jnp.concatenate([a + v2s for a in v1], axis=0)
        tau = _top_values(cand, PEER_TOPK)[-1]
        best = v1[0] + v2[0]
        zsum = jnp.sum(jnp.where(cand >= tau, jnp.exp(cand - best), 0.0), axis=0, keepdims=True)
        s1_ref[h] = s1
        s2_ref[h] = s2
        p1_ref[h] = jnp.exp(s1 - v1[0]) / zsum
        p2_ref[h] = jnp.exp(s2 - v2[0])
        tau_ref[h:h + 1, :] = tau


def _route(q2d, key1, key2):
    rows, d = q2d.shape
    tm = ROUTE_TM
    big = jax.ShapeDtypeStruct((PEER_HEADS, PEER_KEYS, rows), F32)
    bspec = pl.BlockSpec((PEER_HEADS, PEER_KEYS, tm), lambda i: (0, 0, i))
    kspec = pl.BlockSpec((PEER_HEADS, PEER_KEYS, PEER_KEYS), lambda i: (0, 0, 0))
    return pl.pallas_call(
        _route_kernel,
        grid=(rows // tm,),
        in_specs=[pl.BlockSpec((tm, d), lambda i: (i, 0)), kspec, kspec],
        out_specs=[bspec, bspec, bspec, bspec, pl.BlockSpec((PEER_HEADS, tm), lambda i: (0, i))],
        out_shape=[big, big, big, big, jax.ShapeDtypeStruct((PEER_HEADS, rows), F32)],
        compiler_params=_params(("arbitrary",)),
        name="peer_route",
    )(q2d, key1, key2)


PEER_TM = 512
PEER_TE = 512
PEER_TC = 256
GELU_C = 0.7978845608028654


def _experts_kernel(ht_ref, u_ref, vt_ref, s1_ref, s2_ref, p1_ref, p2_ref, tau_ref, o_ref,
                    acc_ref, g_ref):
    e = pl.program_id(1)
    na = PEER_TE // PEER_KEYS

    @pl.when(e == 0)
    def _():
        acc_ref[...] = jnp.zeros_like(acc_ref)

    a_t = _dot(u_ref[...], ht_ref[...])
    for al in range(na):
        a_idx = e * na + al
        for tc in range(PEER_TM // PEER_TC):
            ts = slice(tc * PEER_TC, (tc + 1) * PEER_TC)
            wd = jnp.zeros((PEER_KEYS, PEER_TC), F32)
            for h in range(PEER_HEADS):
                s1r = s1_ref[h, pl.ds(a_idx, 1), ts]
                p1r = p1_ref[h, pl.ds(a_idx, 1), ts]
                score = s1r + s2_ref[h, :, ts]
                wd = wd + jnp.where(score >= tau_ref[h:h + 1, ts], p1r * p2_ref[h, :, ts], 0.0)
            x = a_t[al * PEER_KEYS:(al + 1) * PEER_KEYS, ts]
            gelu = 0.5 * x * (1.0 + jnp.tanh(GELU_C * (x + 0.044715 * (x * x * x))))
            g_ref[al * PEER_KEYS:(al + 1) * PEER_KEYS, ts] = (gelu * wd).astype(BF16)
    acc_ref[...] += _dot(vt_ref[...], g_ref[...])

    @pl.when(e == pl.num_programs(1) - 1)
    def _():
        o_ref[...] = acc_ref[...]


def _experts(h_t, u_bf16, vt_bf16, s1, s2, p1, p2, tau):
    d, rows = h_t.shape
    ne = u_bf16.shape[0]
    tm, te = PEER_TM, PEER_TE
    bspec = pl.BlockSpec((PEER_HEADS, PEER_KEYS, tm), lambda i, e: (0, 0, i))
    return pl.pallas_call(
        _experts_kernel,
        grid=(rows // tm, ne // te),
        in_specs=[pl.BlockSpec((d, tm), lambda i, e: (0, i)),
                  pl.BlockSpec((te, d), lambda i, e: (e, 0)),
                  pl.BlockSpec((d, te), lambda i, e: (0, e)),
                  bspec, bspec, bspec, bspec,
                  pl.BlockSpec((PEER_HEADS, tm), lambda i, e: (0, i))],
        out_specs=pl.BlockSpec((d, tm), lambda i, e: (0, i)),
        out_shape=jax.ShapeDtypeStruct((d, rows), F32),
        scratch_shapes=[pltpu.VMEM((d, tm), F32), pltpu.VMEM((te, tm), BF16)],
        compiler_params=_params(("arbitrary", "arbitrary")),
        name="peer_experts",
    )(h_t, u_bf16, vt_bf16, s1, s2, p1, p2, tau)


def _final_kernel(x_ref, y_ref, m5_ref, g_ref, o_ref):
    x = x_ref[...] + m5_ref[0] * y_ref[...]
    ms = jnp.mean(x * x, axis=-1, keepdims=True)
    o_ref[...] = x * lax.rsqrt(ms + EPS) * g_ref[...]


def _final(x1, y, mods, final_g, seq, tm=512):
    rows, d = x1.shape
    rspec = pl.BlockSpec((tm, d), lambda i: (i, 0))
    return pl.pallas_call(
        _final_kernel,
        grid=(rows // tm,),
        in_specs=[rspec, rspec,
                  pl.BlockSpec((1, 1, d), lambda i: ((i * tm) // seq, 0, 5)),
                  pl.BlockSpec((1, d), lambda i: (0, 0))],
        out_specs=rspec,
        out_shape=jax.ShapeDtypeStruct((rows, d), F32),
        compiler_params=_params(("arbitrary",)),
        name="final",
    )(x1, y, mods, final_g.reshape(1, d))


def kernel(x, c, ctx, c_ctx, w_mod, b_mod, norm1_g, norm2_g, w_in, conv_w, conv_b, conv_ln_g,
           conv_ln_b, w_conv_out, dn_short_w, dn_a_log, dn_dt_bias, dn_norm_g, w_dn_out, w_out,
           peer_w_q, peer_key1, peer_key2, peer_u, peer_v, final_g):
    batch, seq, d = x.shape
    lc = ctx.shape[1]
    assert w_mod.shape[0] == 1, "single-layer block"
    cdim = conv_w.shape[2]
    nqk = DN_HEADS * DN_DK
    col_glu = 2 * cdim
    col_qkv = 3 * nqk
    col_ab = 4 * DN_HEADS
    s_qkv = col_glu
    s_z = s_qkv + col_qkv
    s_ab = s_z + nqk
    s_br = s_ab + col_ab

    c8 = jnp.zeros((8, d), F32).at[:batch].set(c).at[batch].set(c_ctx)
    mods = _mod(c8, w_mod[0], b_mod[0]).reshape(8, 1, N_MOD * d)

    w_in0 = w_in[0]
    w_main = jnp.concatenate([w_in0[:, :s_ab], w_in0[:, s_br:]], axis=1).astype(BF16)
    w_ab = jnp.pad(w_in0[:, s_ab:s_br], ((0, 0), (0, LANES - col_ab)))
    w_ctx = w_in0[:, s_qkv:s_z].astype(BF16)

    x2d = x.reshape(batch * seq, d)
    p, ab_l = _norm_proj(x2d, norm1_g[0], mods, 0, 1, seq, 0, w_main, w_ab=w_ab, name="in_proj")
    pc, ab_c = _norm_proj(ctx.reshape(batch * lc, d), norm1_g[0], mods, 0, 1, batch * lc, batch,
                          w_ctx, w_ab=w_ab, name="in_proj_ctx")

    y_conv = _conformer(p, conv_w[0], conv_b[0], conv_ln_g[0], conv_ln_b[0],
                        w_conv_out[0].astype(BF16), cdim)

    short_w8 = jnp.pad(dn_short_w[0], ((0, 8 - SHORT_CONV), (0, 0)))
    ql, kl, vl = _dn_prep(p, s_qkv // LANES, short_w8, batch, seq)
    qc, kc, vc = _dn_prep(pc, 0, short_w8, batch, lc)
    gate_lanes = lambda t: jnp.pad(
        jnp.pad(t, ((0, 0), (0, DN_HEADS))).reshape(1, col_ab), ((0, 0), (0, LANES - col_ab)))
    o_l = _dn_scan(qc, kc, vc, ab_c.reshape(batch, lc, LANES), ql, kl, vl,
                   ab_l.reshape(batch, seq, LANES), gate_lanes(dn_a_log[0]),
                   gate_lanes(dn_dt_bias[0]))

    zcol = s_z // d
    brcol = (s_br - col_ab) // d
    x1 = _merge(y_conv, o_l.reshape(batch * seq, nqk), p, zcol, brcol,
                jnp.tile(dn_norm_g[0], DN_HEADS).reshape(1, nqk), w_dn_out[0].astype(BF16),
                w_out[0].astype(BF16), x2d, mods, seq)

    q2d, h2 = _norm_proj(x1, norm2_g[0], mods, 3, 4, seq, 0, peer_w_q[0].astype(BF16),
                         with_h=True, name="peer_q")
    s1, s2, p1, p2, tau = _route(q2d, peer_key1[0], peer_key2[0])
    y_t = _experts(h2.T, peer_u[0].astype(BF16), peer_v[0].T.astype(BF16), s1, s2, p1, p2, tau)
    out = _final(x1, y_t.T, mods, final_g, seq)
    return out.reshape(batch, seq, d)
```

```python
import functools

import jax
import jax.numpy as jnp
from jax import lax
from jax.experimental import pallas as pl
from jax.experimental.pallas import tpu as pltpu

F32 = jnp.float32
BF16 = jnp.bfloat16
HI = lax.Precision.HIGHEST

EPS = 1e-6
N_MOD = 6
GRID_W = 64
CONV_WIDTH = 31
DN_HEADS = 16
DN_DK = 128
SHORT_CONV = 5
PEER_HEADS = 8
PEER_KEYS = 128
PEER_TOPK = 16

LANES = 128
DN_CHUNK = 128
NEG_BIG = -3.0e38
VMEM_LIMIT = 56 << 20


def _params(sem):
    return pltpu.CompilerParams(dimension_semantics=sem, vmem_limit_bytes=VMEM_LIMIT)


def _sigmoid(x):
    return 1.0 / (1.0 + jnp.exp(-x))


def _silu(x):
    return x * _sigmoid(x)


def _dot(a, b, precision=None):
    return jnp.dot(a, b, preferred_element_type=F32, precision=precision)


def _dot_nt(a, b, precision=None):
    return lax.dot_general(a, b, (((1,), (1,)), ((), ())), preferred_element_type=F32,
                           precision=precision)


def _split_bf16(a):
    hi = a.astype(BF16)
    lo = (a - hi.astype(F32)).astype(BF16)
    return hi, lo


def _dot3(a, b):
    ah, al = _split_bf16(a)
    bh, bl = _split_bf16(b)
    return _dot(ah, bh) + (_dot(ah, bl) + _dot(al, bh))


def _dot3_nt(a, b):
    ah, al = _split_bf16(a)
    bh, bl = _split_bf16(b)
    return _dot_nt(ah, bh) + (_dot_nt(ah, bl) + _dot_nt(al, bh))


def _mod_kernel(c_ref, w_ref, b_ref, o_ref):
    c = c_ref[...]
    o_ref[...] = _dot(_silu(c), w_ref[...], HI) + b_ref[...]


def _mod(c8, w_mod, b_mod):
    d, n = w_mod.shape
    tn = 1024
    return pl.pallas_call(
        _mod_kernel,
        grid=(n // tn,),
        in_specs=[pl.BlockSpec((8, d), lambda j: (0, 0)),
                  pl.BlockSpec((d, tn), lambda j: (0, j)),
                  pl.BlockSpec((1, tn), lambda j: (0, j))],
        out_specs=pl.BlockSpec((8, tn), lambda j: (0, j)),
        out_shape=jax.ShapeDtypeStruct((8, n), F32),
        compiler_params=_params(("arbitrary",)),
        name="mod",
    )(c8, w_mod, b_mod.reshape(1, n))


def _norm_proj_kernel(x_ref, g_ref, sh_ref, sc_ref, w_ref, *rest, with_ab, with_h):
    rest = list(rest)
    wab_ref = rest.pop(0) if with_ab else None
    o_ref = rest.pop(0)
    oab_ref = rest.pop(0) if with_ab else None
    oh_ref = rest.pop(0) if with_h else None
    h_ref = rest.pop(0)

    @pl.when(pl.program_id(1) == 0)
    def _():
        x = x_ref[...]
        ms = jnp.mean(x * x, axis=-1, keepdims=True)
        y = x * lax.rsqrt(ms + EPS) * g_ref[...]
        h = y * (1.0 + sc_ref[0]) + sh_ref[0]
        hb = h.astype(BF16)
        h_ref[...] = hb
        if with_ab:
            oab_ref[...] = _dot(h, wab_ref[...], HI)
        if with_h:
            oh_ref[...] = h.T.astype(BF16)

    o_ref[...] = _dot(h_ref[...], w_ref[...])


def _norm_proj(x2d, g, mods, shift_chunk, scale_chunk, rows_per_mod, mod_row0, w_bf16, w_ab=None,
               with_h=False, cols=None, tn=512, name="norm_proj"):
    rows, d = x2d.shape
    col0, n = cols if cols is not None else (0, w_bf16.shape[1])
    with_ab = w_ab is not None
    tm = min(1024, rows_per_mod)
    assert rows % tm == 0 and rows_per_mod % tm == 0 and n % tn == 0 and col0 % tn == 0
    jb0 = col0 // tn

    def batch_of_tile(i):
        return mod_row0 + (i * tm) // rows_per_mod

    in_specs = [pl.BlockSpec((tm, d), lambda i, j: (i, 0)),
                pl.BlockSpec((1, d), lambda i, j: (0, 0)),
                pl.BlockSpec((1, 1, d), lambda i, j: (batch_of_tile(i), 0, shift_chunk)),
                pl.BlockSpec((1, 1, d), lambda i, j: (batch_of_tile(i), 0, scale_chunk)),
                pl.BlockSpec((d, tn), lambda i, j: (0, jb0 + j))]
    args = [x2d, g.reshape(1, d), mods, mods, w_bf16]
    out_specs = [pl.BlockSpec((tm, tn), lambda i, j: (i, j))]
    out_shape = [jax.ShapeDtypeStruct((rows, n), F32)]
    if with_ab:
        in_specs.append(pl.BlockSpec((d, LANES), lambda i, j: (0, 0)))
        args.append(w_ab)
        out_specs.append(pl.BlockSpec((tm, LANES), lambda i, j: (i, 0)))
        out_shape.append(jax.ShapeDtypeStruct((rows, LANES), F32))
    if with_h:
        out_specs.append(pl.BlockSpec((d, tm), lambda i, j: (0, i)))
        out_shape.append(jax.ShapeDtypeStruct((d, rows), BF16))
    return pl.pallas_call(
        functools.partial(_norm_proj_kernel, with_ab=with_ab, with_h=with_h),
        grid=(rows // tm, n // tn),
        in_specs=in_specs,
        out_specs=out_specs,
        out_shape=out_shape,
        scratch_shapes=[pltpu.VMEM((tm, d), BF16)],
        compiler_params=_params(("arbitrary", "arbitrary")),
        name=name,
    )(*args)


CONV_ROWS = 4
CONV_CW = 256
CONV_PAD = 16


def _conv_kernel(a_ref, gt_ref, cw_ref, cb_ref, lg_ref, lb_ref, w_ref, o_ref, pad_ref, y_ref,
                 shift_ref):
    c = a_ref.shape[1]
    ncc = c // CONV_CW
    u = a_ref[...] * _sigmoid(gt_ref[...])
    zero = jnp.zeros((CONV_PAD, CONV_CW), F32)
    for r in range(CONV_ROWS):
        for cc in range(ncc):
            pad_ref[r, cc, 0:CONV_PAD, :] = zero
            pad_ref[r, cc, CONV_PAD:CONV_PAD + GRID_W, :] = (
                u[r * GRID_W:(r + 1) * GRID_W, cc * CONV_CW:(cc + 1) * CONV_CW])
            pad_ref[r, cc, CONV_PAD + GRID_W:2 * CONV_PAD + GRID_W, :] = zero

    half = CONV_WIDTH // 2

    sub = 8
    span = GRID_W + 2 * CONV_PAD - sub

    def body(idx, carry):
        r = idx // ncc
        cc = idx % ncc
        for s in range(1, sub):
            shift_ref[s - 1] = pad_ref[r, cc, s:s + span, :]
        acc = jnp.zeros((GRID_W, CONV_CW), F32)
        for k in range(CONV_WIDTH):
            start = CONV_PAD - half + k
            s = start % sub
            base = start - s
            if s == 0:
                src = pad_ref[r, cc, base:base + GRID_W, :]
            else:
                src = shift_ref[s - 1, base:base + GRID_W, :]
            acc = acc + src * cw_ref[cc, k:k + 1, :]
        y_ref[cc, pl.ds(pl.multiple_of(r * GRID_W, GRID_W), GRID_W), :] = acc
        return carry

    lax.fori_loop(0, CONV_ROWS * ncc, body, 0)

    y = jnp.concatenate([y_ref[cc] for cc in range(ncc)], axis=-1) + cb_ref[...]
    mu = jnp.mean(y, axis=-1, keepdims=True)
    yc = y - mu
    var = jnp.mean(yc * yc, axis=-1, keepdims=True)
    yn = yc * lax.rsqrt(var + EPS) * lg_ref[...] + lb_ref[...]
    o_ref[...] = _dot(_silu(yn).astype(BF16), w_ref[...])


def _conformer(p, conv_w, conv_b, ln_g, ln_b, w_out_bf16, c):
    rows = p.shape[0]
    tm = CONV_ROWS * GRID_W
    ncc = c // CONV_CW
    cw = jnp.pad(conv_w, ((0, 32 - CONV_WIDTH), (0, 0)))
    cw = cw.reshape(32, ncc, CONV_CW).transpose(1, 0, 2)
    return pl.pallas_call(
        _conv_kernel,
        grid=(rows // tm,),
        in_specs=[pl.BlockSpec((tm, c), lambda i: (i, 0)),
                  pl.BlockSpec((tm, c), lambda i: (i, 1)),
                  pl.BlockSpec((ncc, 32, CONV_CW), lambda i: (0, 0, 0)),
                  pl.BlockSpec((1, c), lambda i: (0, 0)),
                  pl.BlockSpec((1, c), lambda i: (0, 0)),
                  pl.BlockSpec((1, c), lambda i: (0, 0)),
                  pl.BlockSpec((c, c), lambda i: (0, 0))],
        out_specs=pl.BlockSpec((tm, c), lambda i: (i, 0)),
        out_shape=jax.ShapeDtypeStruct((rows, c), F32),
        scratch_shapes=[pltpu.VMEM((CONV_ROWS, ncc, GRID_W + 2 * CONV_PAD, CONV_CW), F32),
                        pltpu.VMEM((ncc, tm, CONV_CW), F32),
                        pltpu.VMEM((7, GRID_W + 2 * CONV_PAD - 8, CONV_CW), F32)],
        compiler_params=_params(("arbitrary",)),
        name="conformer",
    )(p, p, cw, conv_b.reshape(1, c), ln_g.reshape(1, c), ln_b.reshape(1, c), w_out_bf16)


DN_PAD = 8
DN_RC = 256


def _dnprep_kernel(q_ref, k_ref, v_ref, wq_ref, wk_ref, wv_ref, qo_ref, ko_ref, vo_ref, pad_ref):
    length = q_ref.shape[0]
    zero = jnp.zeros((DN_PAD, LANES), F32)
    half = SHORT_CONV // 2

    def run(x_ref, w_ref, o_ref, mode):
        pad_ref[0:DN_PAD, :] = zero
        pad_ref[DN_PAD + length:2 * DN_PAD + length, :] = zero
        pad_ref[DN_PAD:DN_PAD + length, :] = x_ref[...]
        for r0 in range(0, length, DN_RC):
            acc = jnp.zeros((DN_RC, LANES), F32)
            for k in range(SHORT_CONV):
                start = DN_PAD - half + k + r0
                acc = acc + pad_ref[start:start + DN_RC, :] * w_ref[k:k + 1, :]
            y = _silu(acc)
            if mode != "v":
                y = y * lax.rsqrt(jnp.sum(y * y, axis=-1, keepdims=True) + EPS)
            if mode == "q":
                y = y * (DN_DK ** -0.5)
            o_ref[0, r0:r0 + DN_RC, :] = y.astype(o_ref.dtype)

    run(q_ref, wq_ref, qo_ref, "q")
    run(k_ref, wk_ref, ko_ref, "k")
    run(v_ref, wv_ref, vo_ref, "v")


def _dn_prep(p, col0, short_w8, batch, length):
    nh = DN_HEADS
    out = jax.ShapeDtypeStruct((batch, length, nh * LANES), BF16)
    ospec = pl.BlockSpec((1, length, LANES), lambda b, h: (b, 0, h))
    return pl.pallas_call(
        _dnprep_kernel,
        grid=(batch, nh),
        in_specs=[pl.BlockSpec((length, LANES), lambda b, h: (b, col0 + h)),
                  pl.BlockSpec((length, LANES), lambda b, h: (b, col0 + nh + h)),
                  pl.BlockSpec((length, LANES), lambda b, h: (b, col0 + 2 * nh + h)),
                  pl.BlockSpec((8, LANES), lambda b, h: (0, h)),
                  pl.BlockSpec((8, LANES), lambda b, h: (0, nh + h)),
                  pl.BlockSpec((8, LANES), lambda b, h: (0, 2 * nh + h))],
        out_specs=[ospec, ospec, ospec],
        out_shape=[out, out, out],
        scratch_shapes=[pltpu.VMEM((length + 2 * DN_PAD, LANES), F32)],
        compiler_params=_params(("arbitrary", "arbitrary")),
        name="dn_prep",
    )(p, p, p, short_w8, short_w8, short_w8)


DN_HB = 8
DN_BASE = 8


def _scan_kernel(qc_ref, kc_ref, vc_ref, abc_ref, ql_ref, kl_ref, vl_ref, abl_ref,
                 alog_ref, dtb_ref, o_ref, s_ref):
    ch = DN_CHUNK
    hb = DN_HB
    hg = pl.program_id(1)
    d = pl.program_id(2)
    fwd = d == 0

    row = lax.broadcasted_iota(jnp.int32, (ch, ch), 0)
    col = lax.broadcasted_iota(jnp.int32, (ch, ch), 1)
    diff = jnp.where(fwd, row - col, col - row)
    incl = diff >= 0
    strict = diff > 0
    tri = incl.astype(F32)

    def same_block(size):
        shift = size.bit_length() - 1
        return (row >> shift) == (col >> shift)

    base_mask = same_block(DN_BASE) & strict
    level_masks = []
    size = DN_BASE
    while size < ch:
        level_masks.append(same_block(2 * size) & jnp.logical_not(same_block(size)) & strict)
        size *= 2

    lane0 = 2 * DN_HEADS * d + hg * hb

    neg_a = -jnp.exp(alog_ref[...])
    dtb = dtb_ref[...]

    s_ref[...] = jnp.zeros_like(s_ref)

    heads = range(hb)

    def each(fn, *lists):
        return [fn(*args) for args in zip(*lists)]

    def inverse_minus_identity(ms):
        m0 = each(lambda m: jnp.where(base_mask, m, 0.0), ms)
        m016 = each(lambda t: t.astype(BF16), m0)
        a = each(lambda t: _dot(t, t), m016)
        a16 = each(lambda t: t.astype(BF16), a)
        b = each(lambda t: _dot(t, t), a16)
        f = each(lambda a_, m0_, m16, a16_: a_ - m0_ - _dot(m16, a16_), a, m0, m016, a16)
        e = each(lambda f_, b_: f_ + b_ + _dot(f_.astype(BF16), b_.astype(BF16)), f, b)
        for mask in level_masks:
            c = each(lambda m: jnp.where(mask, m, 0.0), ms)
            tc = each(lambda c_, e_: c_ + _dot(e_.astype(BF16), c_.astype(BF16)), c, e)
            e = each(lambda e_, tc_: e_ - tc_ - _dot(tc_.astype(BF16), e_.astype(BF16)), e, tc)
        return e

    def phase(q_ref, k_ref, v_ref, ab_ref, n_chunks, write_out):
        def body(p, carry):
            ci = jnp.where(fwd, p, n_chunks - 1 - p)
            r0 = pl.multiple_of(ci * ch, ch)
            ab = ab_ref[0, pl.ds(r0, ch), :]
            x = ab + dtb
            softplus = jnp.maximum(x, 0.0) + jnp.log(1.0 + jnp.exp(-jnp.abs(x)))
            g_all = neg_a * softplus
            b_all = _sigmoid(ab)
            gcum_all = _dot(tri, g_all, HI)
            gcum_t = gcum_all.T
            gtot_all = jnp.sum(g_all, axis=0, keepdims=True)
            q16 = [q_ref[0, pl.ds(r0, ch), c * ch:(c + 1) * ch] for c in heads]
            k16 = [k_ref[0, pl.ds(r0, ch), c * ch:(c + 1) * ch] for c in heads]
            q = each(lambda t: t.astype(F32), q16)
            k = each(lambda t: t.astype(F32), k16)
            v = [v_ref[0, pl.ds(r0, ch), c * ch:(c + 1) * ch].astype(F32) for c in heads]
            s = [s_ref[c] for c in heads]
            la = [lane0 + c for c in heads]
            gc_col = [jnp.sum(jnp.where(col == l, gcum_all, 0.0), axis=1, keepdims=True)
                      for l in la]
            beta_col = [jnp.sum(jnp.where(col == l + DN_HEADS, b_all, 0.0), axis=1, keepdims=True)
                        for l in la]
            gc_row = [jnp.sum(jnp.where(row == l, gcum_t, 0.0), axis=0, keepdims=True) for l in la]
            gt = [jnp.sum(jnp.where(col[0:1] == l, gtot_all, 0.0), axis=1, keepdims=True)
                  for l in la]
            decay = each(lambda gc, gr: jnp.where(incl, jnp.exp(gc - gr), 0.0), gc_col, gc_row)
            eg = each(jnp.exp, gc_col)
            if write_out:
                qk = each(lambda q_, k_: _dot_nt(jnp.concatenate([q_, k_], axis=0), k_), q16, k16)
                attn = each(lambda t, dc: t[:ch] * dc, qk, decay)
                kk = each(lambda t: t[ch:], qk)
            else:
                kk = each(lambda k_: _dot_nt(k_, k_), k16)
            m = each(lambda kk_, b_, dc: jnp.where(strict, kk_ * b_ * dc, 0.0), kk, beta_col, decay)
            e = inverse_minus_identity(m)
            rhs = each(lambda v_, k_, b_, eg_: jnp.concatenate([v_ * b_, k_ * (b_ * eg_)], axis=1),
                       v, k, beta_col, eg)
            uw = each(lambda r, e_: r + _dot(e_.astype(BF16), r.astype(BF16)), rhs, e)
            u = each(lambda t: t[:, :LANES], uw)
            w = each(lambda t: t[:, LANES:], uw)
            s16 = each(lambda t: t.astype(BF16), s)
            kdt = each(lambda k_, gt_, gr: k_.T * jnp.exp(gt_ - gr), k, gt, gc_row)
            if write_out:
                ws = each(lambda w_, q_, eg_, s_: _dot(
                    jnp.concatenate([w_, q_ * eg_], axis=0).astype(BF16), s_), w, q, eg, s16)
                v_new = each(lambda u_, ws_: u_ - ws_[:ch], u, ws)
                r2 = each(lambda a_, kd, vn: _dot(
                    jnp.concatenate([a_, kd], axis=0).astype(BF16), vn.astype(BF16)),
                    attn, kdt, v_new)
                o = each(lambda ws_, r2_: ws_[ch:] + r2_[:ch], ws, r2)
                ds = each(lambda t: t[ch:], r2)
            else:
                v_new = each(lambda u_, w_, s_: u_ - _dot(w_.astype(BF16), s_), u, w, s16)
                ds = each(lambda kd, vn: _dot(kd.astype(BF16), vn.astype(BF16)), kdt, v_new)
            s_new = each(lambda s_, gt_, ds_: s_ * jnp.exp(gt_) + ds_, s, gt, ds)
            for c in heads:
                if write_out:
                    o_ref[0, 0, pl.ds(r0, ch), c * ch:(c + 1) * ch] = o[c]
                s_ref[c] = s_new[c]
            return carry

        lax.fori_loop(0, n_chunks, body, 0)

    phase(qc_ref, kc_ref, vc_ref, abc_ref, qc_ref.shape[1] // ch, False)
    phase(ql_ref, kl_ref, vl_ref, abl_ref, ql_ref.shape[1] // ch, True)


def _dn_scan(qc, kc, vc, abc, ql, kl, vl, abl, alog_l, dtb_l):
    batch, lc, hv = qc.shape
    ll = ql.shape[1]
    hb = DN_HB
    w = hb * LANES
    cspec = pl.BlockSpec((1, lc, w), lambda b, g, d: (b, 0, g))
    lspec = pl.BlockSpec((1, ll, w), lambda b, g, d: (b, 0, g))
    return pl.pallas_call(
        _scan_kernel,
        grid=(batch, DN_HEADS // hb, 2),
        in_specs=[cspec, cspec, cspec,
                  pl.BlockSpec((1, lc, LANES), lambda b, g, d: (b, 0, 0)),
                  lspec, lspec, lspec,
                  pl.BlockSpec((1, ll, LANES), lambda b, g, d: (b, 0, 0)),
                  pl.BlockSpec((1, LANES), lambda b, g, d: (0, 0)),
                  pl.BlockSpec((1, LANES), lambda b, g, d: (0, 0))],
        out_specs=pl.BlockSpec((1, 1, ll, w), lambda b, g, d: (d, b, 0, g)),
        out_shape=jax.ShapeDtypeStruct((2, batch, ll, hv), F32),
        scratch_shapes=[pltpu.VMEM((hb, DN_DK, LANES), F32)],
        compiler_params=_params(("arbitrary", "arbitrary", "arbitrary")),
        name="dn_scan",
    )(qc, kc, vc, abc, ql, kl, vl, abl, alog_l, dtb_l)


def _merge_kernel(yc_ref, of_ref, ob_ref, z_ref, brc_ref, brd_ref, g_ref, wdn_ref, wout_ref,
                  x_ref, m2_ref, out_ref):
    o = of_ref[0] + ob_ref[0]
    parts = []
    for h in range(DN_HEADS):
        oh = o[:, h * LANES:(h + 1) * LANES]
        ms = jnp.mean(oh * oh, axis=-1, keepdims=True)
        parts.append(oh * lax.rsqrt(ms + EPS))
    on = jnp.concatenate(parts, axis=-1) * g_ref[...]
    gated = on * _silu(z_ref[...])
    y_dn = _dot(gated.astype(BF16), wdn_ref[...])
    mix = _sigmoid(brc_ref[...]) * yc_ref[...] + _sigmoid(brd_ref[...]) * y_dn
    y = _dot(mix.astype(BF16), wout_ref[...])
    out_ref[...] = x_ref[...] + m2_ref[0] * y


def _merge(y_conv, o2, p, zcol, brcol, norm_g_l, wdn, wout, x2d, mods, seq, tm=256):
    rows, d = x2d.shape
    rspec = lambda cb: pl.BlockSpec((tm, d), lambda i: (i, cb))
    ospec = lambda dr: pl.BlockSpec((1, tm, d), lambda i: (dr, i, 0))
    wspec = pl.BlockSpec((d, d), lambda i: (0, 0), pipeline_mode=pl.Buffered(1))
    return pl.pallas_call(
        _merge_kernel,
        grid=(rows // tm,),
        in_specs=[rspec(0), ospec(0), ospec(1), rspec(zcol), rspec(brcol), rspec(brcol + 1),
                  pl.BlockSpec((1, d), lambda i: (0, 0)), wspec, wspec, rspec(0),
                  pl.BlockSpec((1, 1, d), lambda i: ((i * tm) // seq, 0, 2))],
        out_specs=rspec(0),
        out_shape=jax.ShapeDtypeStruct((rows, d), F32),
        compiler_params=_params(("arbitrary",)),
        name="merge",
    )(y_conv, o2, o2, p, p, p, norm_g_l, wdn, wout, x2d, mods)


ROUTE_TM = 256


def _top_values(vals, count):
    outs = []
    for _ in range(count):
        m = jnp.max(vals, axis=0, keepdims=True)
        outs.append(m)
        vals = jnp.where(vals == m, NEG_BIG, vals)
    return outs


def _route_kernel(q_ref, k1_ref, k2_ref, s1_ref, s2_ref, p1_ref, p2_ref, tau_ref):
    q = q_ref[...]
    half = PEER_KEYS
    for h in range(PEER_HEADS):
        q1 = q[:, (2 * h) * half:(2 * h + 1) * half]
        q2 = q[:, (2 * h + 1) * half:(2 * h + 2) * half]
        s1 = _dot_nt(k1_ref[h], q1, HI)
        s2 = _dot_nt(k2_ref[h], q2, HI)
        v1 = _top_values(s1, PEER_TOPK)
        v2 = _top_values(s2, PEER_TOPK)
        v2s = jnp.concatenate(v2, axis=0)
        pieces = [v1[a] + v2s[0:PEER_TOPK // (a + 1)] for a in range(PEER_TOPK)]
        n_cand = sum(p.shape[0] for p in pieces)
        pad = -n_cand % 8
        if pad:
            pieces.append(jnp.full((pad, v2s.shape[1]), NEG_BIG, F32))
        cand = jnp.concatenate(pieces, axis=0)
        tau = _top_values(cand, PEER_TOPK)[-1]
        best = v1[0] + v2[0]
        zsum = jnp.sum(jnp.where(cand >= tau, jnp.exp(cand - best), 0.0), axis=0, keepdims=True)
        s1_ref[h] = s1
        s2_ref[h] = s2
        p1_ref[h] = jnp.exp(s1 - v1[0]) / zsum
        p2_ref[h] = jnp.exp(s2 - v2[0])
        tau_ref[h:h + 1, :] = tau


def _route(q2d, key1, key2):
    rows, d = q2d.shape
    tm = ROUTE_TM
    big = jax.ShapeDtypeStruct((PEER_HEADS, PEER_KEYS, rows), F32)
    bspec = pl.BlockSpec((PEER_HEADS, PEER_KEYS, tm), lambda i: (0, 0, i))
    kspec = pl.BlockSpec((PEER_HEADS, PEER_KEYS, PEER_KEYS), lambda i: (0, 0, 0))
    return pl.pallas_call(
        _route_kernel,
        grid=(rows // tm,),
        in_specs=[pl.BlockSpec((tm, d), lambda i: (i, 0)), kspec, kspec],
        out_specs=[bspec, bspec, bspec, bspec, pl.BlockSpec((PEER_HEADS, tm), lambda i: (0, i))],
        out_shape=[big, big, big, big, jax.ShapeDtypeStruct((PEER_HEADS, rows), F32)],
        compiler_params=_params(("arbitrary",)),
        name="peer_route",
    )(q2d, key1, key2)


PEER_TM = 512
PEER_TE = 512
PEER_TC = 256
PEER_WC = 256
PEER_WH = 1
PEER_AM = 128
PEER_BSPLIT = 8
GELU_C = 0.7978845608028654


def _experts_kernel(ht_ref, u_ref, vt_ref, s1_ref, s2_ref, p1_ref, p2_ref, tau_ref, o_ref,
                    acc_ref, a0_ref, a1_ref, g0_ref, g1_ref):
    t = pl.program_id(1)
    na = PEER_TE // PEER_KEYS

    @pl.when(t == 0)
    def _():
        acc_ref[...] = jnp.zeros_like(acc_ref)
        for ref in (a0_ref, a1_ref, g0_ref, g1_ref):
            ref[...] = jnp.zeros_like(ref)

    ntc = PEER_TM // PEER_TC
    d_model = acc_ref.shape[0]
    mb = d_model // PEER_BSPLIT

    def tick(a_cur, a_prev, g_cur, g_prev, row0):
        def a_piece(m, n):
            rows = slice(m * PEER_AM, (m + 1) * PEER_AM)
            ts = slice(n * PEER_TC, (n + 1) * PEER_TC)
            a_cur[rows, ts] = _dot(u_ref[rows, :], ht_ref[:, ts])

        def b_piece(m):
            rows = slice(m * mb, (m + 1) * mb)
            acc_ref[rows, :] += _dot(vt_ref[rows, :], g_cur[...])

        def w_stage():
            for al in range(na):
                rows = slice(al * PEER_KEYS, (al + 1) * PEER_KEYS)
                for tc in range(PEER_TM // PEER_WC):
                    ts = slice(tc * PEER_WC, (tc + 1) * PEER_WC)
                    wd = jnp.zeros((PEER_KEYS, PEER_WC), F32)
                    for h in range(PEER_HEADS):
                        s1r = s1_ref[h, row0 + al:row0 + al + 1, ts]
                        p1r = p1_ref[h, row0 + al:row0 + al + 1, ts]
                        score = s1r + s2_ref[h, :, ts]
                        wd = wd + jnp.where(score >= tau_ref[h:h + 1, ts],
                                            p1r * p2_ref[h, :, ts], 0.0)
                        if h % PEER_WH == PEER_WH - 1 and h < PEER_HEADS - 1:
                            yield
                    x = a_prev[rows, ts]
                    inner = x * (GELU_C + (GELU_C * 0.044715) * (x * x))
                    hx = 0.5 * x
                    g_prev[rows, ts] = ((hx + hx * jnp.tanh(inner)) * wd).astype(BF16)
                    yield

        mxu = [functools.partial(a_piece, m, n) for m in range(PEER_TE // PEER_AM)
               for n in range(ntc)]
        mxu += [functools.partial(b_piece, m) for m in range(PEER_BSPLIT)]
        n_w = na * (PEER_TM // PEER_WC) * (PEER_HEADS // PEER_WH)
        per = -(-n_w // len(mxu))
        w_steps = w_stage()
        for piece in mxu:
            piece()
            for _ in range(per):
                next(w_steps, None)
        for _ in w_steps:
            pass

    @pl.when(t % 2 == 0)
    def _():
        tick(a0_ref, a1_ref, g0_ref, g1_ref, na)

    @pl.when(t % 2 == 1)
    def _():
        tick(a1_ref, a0_ref, g1_ref, g0_ref, 0)

    @pl.when(t == pl.num_programs(1) - 1)
    def _():
        o_ref[...] = acc_ref[...]


def _experts(h_t, u_bf16, vt_bf16, s1, s2, p1, p2, tau):
    d, rows = h_t.shape
    ne = u_bf16.shape[0]
    tm, te = PEER_TM, PEER_TE
    nblk = ne // te
    na = te // PEER_KEYS
    bspec = pl.BlockSpec((PEER_HEADS, PEER_KEYS, tm), lambda i, t: (0, 0, i))
    assert 2 * na == 8 and nblk % 2 == 0
    aspec = pl.BlockSpec((PEER_HEADS, 2 * na, tm),
                         lambda i, t: (0, jnp.clip(t - 1, 0, nblk - 1) // 2, i))
    return pl.pallas_call(
        _experts_kernel,
        grid=(rows // tm, nblk + 2),
        in_specs=[pl.BlockSpec((d, tm), lambda i, t: (0, i)),
                  pl.BlockSpec((te, d), lambda i, t: (jnp.minimum(t, nblk - 1), 0)),
                  pl.BlockSpec((d, te), lambda i, t: (0, jnp.clip(t - 2, 0, nblk - 1))),
                  aspec, bspec, aspec, bspec,
                  pl.BlockSpec((PEER_HEADS, tm), lambda i, t: (0, i))],
        out_specs=pl.BlockSpec((d, tm), lambda i, t: (0, i)),
        out_shape=jax.ShapeDtypeStruct((d, rows), F32),
        scratch_shapes=[pltpu.VMEM((d, tm), F32), pltpu.VMEM((te, tm), F32),
                        pltpu.VMEM((te, tm), F32), pltpu.VMEM((te, tm), BF16),
                        pltpu.VMEM((te, tm), BF16)],
        compiler_params=_params(("arbitrary", "arbitrary")),
        name="peer_experts",
    )(h_t, u_bf16, vt_bf16, s1, s2, p1, p2, tau)


def _final_kernel(x_ref, yt_ref, m5_ref, g_ref, o_ref):
    x = x_ref[...] + m5_ref[0] * yt_ref[...].T
    ms = jnp.mean(x * x, axis=-1, keepdims=True)
    o_ref[...] = x * lax.rsqrt(ms + EPS) * g_ref[...]


def _final(x1, y_t, mods, final_g, seq, tm=512):
    rows, d = x1.shape
    rspec = pl.BlockSpec((tm, d), lambda i: (i, 0))
    return pl.pallas_call(
        _final_kernel,
        grid=(rows // tm,),
        in_specs=[rspec, pl.BlockSpec((d, tm), lambda i: (0, i)),
                  pl.BlockSpec((1, 1, d), lambda i: ((i * tm) // seq, 0, 5)),
                  pl.BlockSpec((1, d), lambda i: (0, 0))],
        out_specs=rspec,
        out_shape=jax.ShapeDtypeStruct((rows, d), F32),
        compiler_params=_params(("arbitrary",)),
        name="final",
    )(x1, y_t, mods, final_g.reshape(1, d))


def kernel(x, c, ctx, c_ctx, w_mod, b_mod, norm1_g, norm2_g, w_in, conv_w, conv_b, conv_ln_g,
           conv_ln_b, w_conv_out, dn_short_w, dn_a_log, dn_dt_bias, dn_norm_g, w_dn_out, w_out,
           peer_w_q, peer_key1, peer_key2, peer_u, peer_v, final_g):
    batch, seq, d = x.shape
    lc = ctx.shape[1]
    assert w_mod.shape[0] == 1, "single-layer block"
    cdim = conv_w.shape[2]
    nqk = DN_HEADS * DN_DK
    col_glu = 2 * cdim
    col_qkv = 3 * nqk
    col_ab = 4 * DN_HEADS
    s_qkv = col_glu
    s_z = s_qkv + col_qkv
    s_ab = s_z + nqk
    s_br = s_ab + col_ab

    c8 = jnp.zeros((8, d), F32).at[:batch].set(c).at[batch].set(c_ctx)
    mods = _mod(c8, w_mod[0], b_mod[0]).reshape(8, 1, N_MOD * d)

    w_in0 = w_in[0]
    w_main = jnp.concatenate([w_in0[:, :s_ab], w_in0[:, s_br:]], axis=1).astype(BF16)
    w_ab = jnp.pad(w_in0[:, s_ab:s_br], ((0, 0), (0, LANES - col_ab)))

    x2d = x.reshape(batch * seq, d)
    p, ab_l = _norm_proj(x2d, norm1_g[0], mods, 0, 1, seq, 0, w_main, w_ab=w_ab, name="in_proj")
    pc, ab_c = _norm_proj(ctx.reshape(batch * lc, d), norm1_g[0], mods, 0, 1, batch * lc, batch,
                          w_main, w_ab=w_ab, cols=(s_qkv, col_qkv), name="in_proj_ctx")

    y_conv = _conformer(p, conv_w[0], conv_b[0], conv_ln_g[0], conv_ln_b[0],
                        w_conv_out[0].astype(BF16), cdim)

    short_w8 = jnp.pad(dn_short_w[0], ((0, 8 - SHORT_CONV), (0, 0)))
    ql, kl, vl = _dn_prep(p, s_qkv // LANES, short_w8, batch, seq)
    qc, kc, vc = _dn_prep(pc, 0, short_w8, batch, lc)
    gate_lanes = lambda t: jnp.pad(
        jnp.pad(t, ((0, 0), (0, DN_HEADS))).reshape(1, col_ab), ((0, 0), (0, LANES - col_ab)))
    o_l = _dn_scan(qc, kc, vc, ab_c.reshape(batch, lc, LANES), ql, kl, vl,
                   ab_l.reshape(batch, seq, LANES), gate_lanes(dn_a_log[0]),
                   gate_lanes(dn_dt_bias[0]))

    zcol = s_z // d
    brcol = (s_br - col_ab) // d
    x1 = _merge(y_conv, o_l.reshape(2, batch * seq, nqk), p, zcol, brcol,
                jnp.tile(dn_norm_g[0], DN_HEADS).reshape(1, nqk), w_dn_out[0].astype(BF16),
                w_out[0].astype(BF16), x2d, mods, seq)

    q2d, h2_t = _norm_proj(x1, norm2_g[0], mods, 3, 4, seq, 0, peer_w_q[0].astype(BF16),
                           with_h=True, name="peer_q")
    s1, s2, p1, p2, tau = _route(q2d, peer_key1[0], peer_key2[0])
    y_t = _experts(h2_t, peer_u[0].astype(BF16), peer_v[0].T.astype(BF16), s1, s2, p1, p2, tau)
    out = _final(x1, y_t, mods, final_g, seq)
    return out.reshape(batch, seq, d)
```

```python
import functools

import jax
import jax.numpy as jnp
from jax import lax
from jax.experimental import pallas as pl
from jax.experimental.pallas import tpu as pltpu

F32 = jnp.float32
BF16 = jnp.bfloat16
HI = lax.Precision.HIGHEST

EPS = 1e-6
N_MOD = 6
GRID_W = 64
CONV_WIDTH = 31
DN_HEADS = 16
DN_DK = 128
SHORT_CONV = 5
PEER_HEADS = 8
PEER_KEYS = 128
PEER_TOPK = 16

LANES = 128
DN_CHUNK = 128
NEG_BIG = -3.0e38
VMEM_LIMIT = 56 << 20


def _params(sem):
    return pltpu.CompilerParams(dimension_semantics=sem, vmem_limit_bytes=VMEM_LIMIT)


def _sigmoid(x):
    return 1.0 / (1.0 + jnp.exp(-x))


def _silu(x):
    return x * _sigmoid(x)


def _dot(a, b, precision=None):
    return jnp.dot(a, b, preferred_element_type=F32, precision=precision)


def _dot_nt(a, b, precision=None):
    return lax.dot_general(a, b, (((1,), (1,)), ((), ())), preferred_element_type=F32,
                           precision=precision)


def _split_bf16(a):
    hi = a.astype(BF16)
    lo = (a - hi.astype(F32)).astype(BF16)
    return hi, lo


def _dot3(a, b):
    ah, al = _split_bf16(a)
    bh, bl = _split_bf16(b)
    return _dot(ah, bh) + (_dot(ah, bl) + _dot(al, bh))


def _dot3_nt(a, b):
    ah, al = _split_bf16(a)
    bh, bl = _split_bf16(b)
    return _dot_nt(ah, bh) + (_dot_nt(ah, bl) + _dot_nt(al, bh))


def _mod_kernel(c_ref, w_ref, b_ref, o_ref):
    c = c_ref[...]
    o_ref[...] = _dot(_silu(c), w_ref[...], HI) + b_ref[...]


def _mod(c8, w_mod, b_mod):
    d, n = w_mod.shape
    tn = 1024
    return pl.pallas_call(
        _mod_kernel,
        grid=(n // tn,),
        in_specs=[pl.BlockSpec((8, d), lambda j: (0, 0)),
                  pl.BlockSpec((d, tn), lambda j: (0, j)),
                  pl.BlockSpec((1, tn), lambda j: (0, j))],
        out_specs=pl.BlockSpec((8, tn), lambda j: (0, j)),
        out_shape=jax.ShapeDtypeStruct((8, n), F32),
        compiler_params=_params(("arbitrary",)),
        name="mod",
    )(c8, w_mod, b_mod.reshape(1, n))


def _norm_proj_kernel(x_ref, g_ref, sh_ref, sc_ref, w_ref, *rest, with_ab, with_h):
    rest = list(rest)
    wab_ref = rest.pop(0) if with_ab else None
    o_ref = rest.pop(0)
    oab_ref = rest.pop(0) if with_ab else None
    oh_ref = rest.pop(0) if with_h else None
    h_ref = rest.pop(0)

    @pl.when(pl.program_id(1) == 0)
    def _():
        x = x_ref[...]
        ms = jnp.mean(x * x, axis=-1, keepdims=True)
        y = x * lax.rsqrt(ms + EPS) * g_ref[...]
        h = y * (1.0 + sc_ref[0]) + sh_ref[0]
        hb = h.astype(BF16)
        h_ref[...] = hb
        if with_ab:
            oab_ref[...] = _dot(h, wab_ref[...], HI)
        if with_h:
            oh_ref[...] = h.T.astype(BF16)

    o_ref[...] = _dot(h_ref[...], w_ref[...])


def _norm_proj(x2d, g, mods, shift_chunk, scale_chunk, rows_per_mod, mod_row0, w_bf16, w_ab=None,
               with_h=False, cols=None, tn=512, name="norm_proj"):
    rows, d = x2d.shape
    col0, n = cols if cols is not None else (0, w_bf16.shape[1])
    with_ab = w_ab is not None
    tm = min(1024, rows_per_mod)
    assert rows % tm == 0 and rows_per_mod % tm == 0 and n % tn == 0 and col0 % tn == 0
    jb0 = col0 // tn

    def batch_of_tile(i):
        return mod_row0 + (i * tm) // rows_per_mod

    in_specs = [pl.BlockSpec((tm, d), lambda i, j: (i, 0)),
                pl.BlockSpec((1, d), lambda i, j: (0, 0)),
                pl.BlockSpec((1, 1, d), lambda i, j: (batch_of_tile(i), 0, shift_chunk)),
                pl.BlockSpec((1, 1, d), lambda i, j: (batch_of_tile(i), 0, scale_chunk)),
                pl.BlockSpec((d, tn), lambda i, j: (0, jb0 + j))]
    args = [x2d, g.reshape(1, d), mods, mods, w_bf16]
    out_specs = [pl.BlockSpec((tm, tn), lambda i, j: (i, j))]
    out_shape = [jax.ShapeDtypeStruct((rows, n), F32)]
    if with_ab:
        in_specs.append(pl.BlockSpec((d, LANES), lambda i, j: (0, 0)))
        args.append(w_ab)
        out_specs.append(pl.BlockSpec((tm, LANES), lambda i, j: (i, 0)))
        out_shape.append(jax.ShapeDtypeStruct((rows, LANES), F32))
    if with_h:
        out_specs.append(pl.BlockSpec((d, tm), lambda i, j: (0, i)))
        out_shape.append(jax.ShapeDtypeStruct((d, rows), BF16))
    return pl.pallas_call(
        functools.partial(_norm_proj_kernel, with_ab=with_ab, with_h=with_h),
        grid=(rows // tm, n // tn),
        in_specs=in_specs,
        out_specs=out_specs,
        out_shape=out_shape,
        scratch_shapes=[pltpu.VMEM((tm, d), BF16)],
        compiler_params=_params(("arbitrary", "arbitrary")),
        name=name,
    )(*args)


CONV_ROWS = 4
CONV_CW = 256
CONV_PAD = 16


def _conv_kernel(a_ref, gt_ref, cw_ref, cb_ref, lg_ref, lb_ref, w_ref, o_ref, pad_ref, y_ref,
                 shift_ref):
    c = a_ref.shape[1]
    ncc = c // CONV_CW
    u = a_ref[...] * _sigmoid(gt_ref[...])
    zero = jnp.zeros((CONV_PAD, CONV_CW), F32)
    for r in range(CONV_ROWS):
        for cc in range(ncc):
            pad_ref[r, cc, 0:CONV_PAD, :] = zero
            pad_ref[r, cc, CONV_PAD:CONV_PAD + GRID_W, :] = (
                u[r * GRID_W:(r + 1) * GRID_W, cc * CONV_CW:(cc + 1) * CONV_CW])
            pad_ref[r, cc, CONV_PAD + GRID_W:2 * CONV_PAD + GRID_W, :] = zero

    half = CONV_WIDTH // 2

    sub = 8
    span = GRID_W + 2 * CONV_PAD - sub

    def body(idx, carry):
        r = idx // ncc
        cc = idx % ncc
        for s in range(1, sub):
            shift_ref[s - 1] = pad_ref[r, cc, s:s + span, :]
        acc = jnp.zeros((GRID_W, CONV_CW), F32)
        for k in range(CONV_WIDTH):
            start = CONV_PAD - half + k
            s = start % sub
            base = start - s
            if s == 0:
                src = pad_ref[r, cc, base:base + GRID_W, :]
            else:
                src = shift_ref[s - 1, base:base + GRID_W, :]
            acc = acc + src * cw_ref[cc, k:k + 1, :]
        y_ref[cc, pl.ds(pl.multiple_of(r * GRID_W, GRID_W), GRID_W), :] = acc
        return carry

    lax.fori_loop(0, CONV_ROWS * ncc, body, 0)

    y = jnp.concatenate([y_ref[cc] for cc in range(ncc)], axis=-1) + cb_ref[...]
    mu = jnp.mean(y, axis=-1, keepdims=True)
    yc = y - mu
    var = jnp.mean(yc * yc, axis=-1, keepdims=True)
    yn = yc * lax.rsqrt(var + EPS) * lg_ref[...] + lb_ref[...]
    o_ref[...] = _dot(_silu(yn).astype(BF16), w_ref[...])


def _conformer(p, conv_w, conv_b, ln_g, ln_b, w_out_bf16, c):
    rows = p.shape[0]
    tm = CONV_ROWS * GRID_W
    ncc = c // CONV_CW
    cw = jnp.pad(conv_w, ((0, 32 - CONV_WIDTH), (0, 0)))
    cw = cw.reshape(32, ncc, CONV_CW).transpose(1, 0, 2)
    return pl.pallas_call(
        _conv_kernel,
        grid=(rows // tm,),
        in_specs=[pl.BlockSpec((tm, c), lambda i: (i, 0)),
                  pl.BlockSpec((tm, c), lambda i: (i, 1)),
                  pl.BlockSpec((ncc, 32, CONV_CW), lambda i: (0, 0, 0)),
                  pl.BlockSpec((1, c), lambda i: (0, 0)),
                  pl.BlockSpec((1, c), lambda i: (0, 0)),
                  pl.BlockSpec((1, c), lambda i: (0, 0)),
                  pl.BlockSpec((c, c), lambda i: (0, 0))],
        out_specs=pl.BlockSpec((tm, c), lambda i: (i, 0)),
        out_shape=jax.ShapeDtypeStruct((rows, c), F32),
        scratch_shapes=[pltpu.VMEM((CONV_ROWS, ncc, GRID_W + 2 * CONV_PAD, CONV_CW), F32),
                        pltpu.VMEM((ncc, tm, CONV_CW), F32),
                        pltpu.VMEM((7, GRID_W + 2 * CONV_PAD - 8, CONV_CW), F32)],
        compiler_params=_params(("arbitrary",)),
        name="conformer",
    )(p, p, cw, conv_b.reshape(1, c), ln_g.reshape(1, c), ln_b.reshape(1, c), w_out_bf16)


DN_PAD = 8
DN_RC = 256


def _dnprep_kernel(q_ref, k_ref, v_ref, wq_ref, wk_ref, wv_ref, qo_ref, ko_ref, vo_ref, pad_ref):
    length = q_ref.shape[0]
    zero = jnp.zeros((DN_PAD, LANES), F32)
    half = SHORT_CONV // 2

    def run(x_ref, w_ref, o_ref, mode):
        pad_ref[0:DN_PAD, :] = zero
        pad_ref[DN_PAD + length:2 * DN_PAD + length, :] = zero
        pad_ref[DN_PAD:DN_PAD + length, :] = x_ref[...]
        for r0 in range(0, length, DN_RC):
            acc = jnp.zeros((DN_RC, LANES), F32)
            for k in range(SHORT_CONV):
                start = DN_PAD - half + k + r0
                acc = acc + pad_ref[start:start + DN_RC, :] * w_ref[k:k + 1, :]
            y = _silu(acc)
            if mode != "v":
                y = y * lax.rsqrt(jnp.sum(y * y, axis=-1, keepdims=True) + EPS)
            if mode == "q":
                y = y * (DN_DK ** -0.5)
            o_ref[0, r0:r0 + DN_RC, :] = y.astype(o_ref.dtype)

    run(q_ref, wq_ref, qo_ref, "q")
    run(k_ref, wk_ref, ko_ref, "k")
    run(v_ref, wv_ref, vo_ref, "v")


def _dn_prep(p, col0, short_w8, batch, length):
    nh = DN_HEADS
    out = jax.ShapeDtypeStruct((batch, length, nh * LANES), BF16)
    ospec = pl.BlockSpec((1, length, LANES), lambda b, h: (b, 0, h))
    return pl.pallas_call(
        _dnprep_kernel,
        grid=(batch, nh),
        in_specs=[pl.BlockSpec((length, LANES), lambda b, h: (b, col0 + h)),
                  pl.BlockSpec((length, LANES), lambda b, h: (b, col0 + nh + h)),
                  pl.BlockSpec((length, LANES), lambda b, h: (b, col0 + 2 * nh + h)),
                  pl.BlockSpec((8, LANES), lambda b, h: (0, h)),
                  pl.BlockSpec((8, LANES), lambda b, h: (0, nh + h)),
                  pl.BlockSpec((8, LANES), lambda b, h: (0, 2 * nh + h))],
        out_specs=[ospec, ospec, ospec],
        out_shape=[out, out, out],
        scratch_shapes=[pltpu.VMEM((length + 2 * DN_PAD, LANES), F32)],
        compiler_params=_params(("arbitrary", "arbitrary")),
        name="dn_prep",
    )(p, p, p, short_w8, short_w8, short_w8)


DN_HB = 8
DN_BASE = 8


def _scan_kernel(qc_ref, kc_ref, vc_ref, abc_ref, ql_ref, kl_ref, vl_ref, abl_ref,
                 alog_ref, dtb_ref, o_ref, s_ref):
    ch = DN_CHUNK
    hb = DN_HB
    hg = pl.program_id(1)
    d = pl.program_id(2)
    fwd = d == 0

    row = lax.broadcasted_iota(jnp.int32, (ch, ch), 0)
    col = lax.broadcasted_iota(jnp.int32, (ch, ch), 1)
    diff = jnp.where(fwd, row - col, col - row)
    incl = diff >= 0
    strict = diff > 0
    tri = incl.astype(F32)

    def same_block(size):
        shift = size.bit_length() - 1
        return (row >> shift) == (col >> shift)

    base_mask = same_block(DN_BASE) & strict
    level_masks = []
    size = DN_BASE
    while size < ch:
        level_masks.append(same_block(2 * size) & jnp.logical_not(same_block(size)) & strict)
        size *= 2

    lane0 = 2 * DN_HEADS * d + hg * hb

    neg_a = -jnp.exp(alog_ref[...])
    dtb = dtb_ref[...]

    s_ref[...] = jnp.zeros_like(s_ref)

    heads = range(hb)

    def each(fn, *lists):
        return [fn(*args) for args in zip(*lists)]

    def inverse_minus_identity(ms):
        m0 = each(lambda m: jnp.where(base_mask, m, 0.0), ms)
        m016 = each(lambda t: t.astype(BF16), m0)
        a = each(lambda t: _dot(t, t), m016)
        a16 = each(lambda t: t.astype(BF16), a)
        b = each(lambda t: _dot(t, t), a16)
        f = each(lambda a_, m0_, m16, a16_: a_ - m0_ - _dot(m16, a16_), a, m0, m016, a16)
        e = each(lambda f_, b_: f_ + b_ + _dot(f_.astype(BF16), b_.astype(BF16)), f, b)
        for mask in level_masks:
            c = each(lambda m: jnp.where(mask, m, 0.0), ms)
            tc = each(lambda c_, e_: c_ + _dot(e_.astype(BF16), c_.astype(BF16)), c, e)
            e = each(lambda e_, tc_: e_ - tc_ - _dot(tc_.astype(BF16), e_.astype(BF16)), e, tc)
        return e

    def phase(q_ref, k_ref, v_ref, ab_ref, n_chunks, write_out):
        def body(p, carry):
            ci = jnp.where(fwd, p, n_chunks - 1 - p)
            r0 = pl.multiple_of(ci * ch, ch)
            ab = ab_ref[0, pl.ds(r0, ch), :]
            x = ab + dtb
            softplus = jnp.maximum(x, 0.0) + jnp.log(1.0 + jnp.exp(-jnp.abs(x)))
            g_all = neg_a * softplus
            b_all = _sigmoid(ab)
            gcum_all = _dot(tri, g_all, HI)
            gcum_t = gcum_all.T
            gtot_all = jnp.sum(g_all, axis=0, keepdims=True)
            q16 = [q_ref[0, pl.ds(r0, ch), c * ch:(c + 1) * ch] for c in heads]
            k16 = [k_ref[0, pl.ds(r0, ch), c * ch:(c + 1) * ch] for c in heads]
            q = each(lambda t: t.astype(F32), q16)
            k = each(lambda t: t.astype(F32), k16)
            v = [v_ref[0, pl.ds(r0, ch), c * ch:(c + 1) * ch].astype(F32) for c in heads]
            s = [s_ref[c] for c in heads]
            la = [lane0 + c for c in heads]
            gc_col = [jnp.sum(jnp.where(col == l, gcum_all, 0.0), axis=1, keepdims=True)
                      for l in la]
            beta_col = [jnp.sum(jnp.where(col == l + DN_HEADS, b_all, 0.0), axis=1, keepdims=True)
                        for l in la]
            gc_row = [jnp.sum(jnp.where(row == l, gcum_t, 0.0), axis=0, keepdims=True) for l in la]
            gt = [jnp.sum(jnp.where(col[0:1] == l, gtot_all, 0.0), axis=1, keepdims=True)
                  for l in la]
            decay = each(lambda gc, gr: jnp.where(incl, jnp.exp(gc - gr), 0.0), gc_col, gc_row)
            eg = each(jnp.exp, gc_col)
            if write_out:
                qk = each(lambda q_, k_: _dot_nt(jnp.concatenate([q_, k_], axis=0), k_), q16, k16)
                attn = each(lambda t, dc: t[:ch] * dc, qk, decay)
                kk = each(lambda t: t[ch:], qk)
            else:
                kk = each(lambda k_: _dot_nt(k_, k_), k16)
            m = each(lambda kk_, b_, dc: jnp.where(strict, kk_ * b_ * dc, 0.0), kk, beta_col, decay)
            e = inverse_minus_identity(m)
            rhs = each(lambda v_, k_, b_, eg_: jnp.concatenate([v_ * b_, k_ * (b_ * eg_)], axis=1),
                       v, k, beta_col, eg)
            uw = each(lambda r, e_: r + _dot(e_.astype(BF16), r.astype(BF16)), rhs, e)
            u = each(lambda t: t[:, :LANES], uw)
            w = each(lambda t: t[:, LANES:], uw)
            s16 = each(lambda t: t.astype(BF16), s)
            kdt = each(lambda k_, gt_, gr: k_.T * jnp.exp(gt_ - gr), k, gt, gc_row)
            if write_out:
                ws = each(lambda w_, q_, eg_, s_: _dot(
                    jnp.concatenate([w_, q_ * eg_], axis=0).astype(BF16), s_), w, q, eg, s16)
                v_new = each(lambda u_, ws_: u_ - ws_[:ch], u, ws)
                r2 = each(lambda a_, kd, vn: _dot(
                    jnp.concatenate([a_, kd], axis=0).astype(BF16), vn.astype(BF16)),
                    attn, kdt, v_new)
                o = each(lambda ws_, r2_: ws_[ch:] + r2_[:ch], ws, r2)
                ds = each(lambda t: t[ch:], r2)
            else:
                v_new = each(lambda u_, w_, s_: u_ - _dot(w_.astype(BF16), s_), u, w, s16)
                ds = each(lambda kd, vn: _dot(kd.astype(BF16), vn.astype(BF16)), kdt, v_new)
            s_new = each(lambda s_, gt_, ds_: s_ * jnp.exp(gt_) + ds_, s, gt, ds)
            for c in heads:
                if write_out:
                    o_ref[0, 0, pl.ds(r0, ch), c * ch:(c + 1) * ch] = o[c]
                s_ref[c] = s_new[c]
            return carry

        lax.fori_loop(0, n_chunks, body, 0)

    phase(qc_ref, kc_ref, vc_ref, abc_ref, qc_ref.shape[1] // ch, False)
    phase(ql_ref, kl_ref, vl_ref, abl_ref, ql_ref.shape[1] // ch, True)


def _dn_scan(qc, kc, vc, abc, ql, kl, vl, abl, alog_l, dtb_l):
    batch, lc, hv = qc.shape
    ll = ql.shape[1]
    hb = DN_HB
    w = hb * LANES
    cspec = pl.BlockSpec((1, lc, w), lambda b, g, d: (b, 0, g))
    lspec = pl.BlockSpec((1, ll, w), lambda b, g, d: (b, 0, g))
    return pl.pallas_call(
        _scan_kernel,
        grid=(batch, DN_HEADS // hb, 2),
        in_specs=[cspec, cspec, cspec,
                  pl.BlockSpec((1, lc, LANES), lambda b, g, d: (b, 0, 0)),
                  lspec, lspec, lspec,
                  pl.BlockSpec((1, ll, LANES), lambda b, g, d: (b, 0, 0)),
                  pl.BlockSpec((1, LANES), lambda b, g, d: (0, 0)),
                  pl.BlockSpec((1, LANES), lambda b, g, d: (0, 0))],
        out_specs=pl.BlockSpec((1, 1, ll, w), lambda b, g, d: (d, b, 0, g)),
        out_shape=jax.ShapeDtypeStruct((2, batch, ll, hv), F32),
        scratch_shapes=[pltpu.VMEM((hb, DN_DK, LANES), F32)],
        compiler_params=_params(("arbitrary", "arbitrary", "arbitrary")),
        name="dn_scan",
    )(qc, kc, vc, abc, ql, kl, vl, abl, alog_l, dtb_l)


def _merge_kernel(yc_ref, of_ref, ob_ref, z_ref, brc_ref, brd_ref, g_ref, wdn_ref, wout_ref,
                  x_ref, m2_ref, out_ref):
    o = of_ref[0] + ob_ref[0]
    parts = []
    for h in range(DN_HEADS):
        oh = o[:, h * LANES:(h + 1) * LANES]
        ms = jnp.mean(oh * oh, axis=-1, keepdims=True)
        parts.append(oh * lax.rsqrt(ms + EPS))
    on = jnp.concatenate(parts, axis=-1) * g_ref[...]
    gated = on * _silu(z_ref[...])
    y_dn = _dot(gated.astype(BF16), wdn_ref[...])
    mix = _sigmoid(brc_ref[...]) * yc_ref[...] + _sigmoid(brd_ref[...]) * y_dn
    y = _dot(mix.astype(BF16), wout_ref[...])
    out_ref[...] = x_ref[...] + m2_ref[0] * y


def _merge(y_conv, o2, p, zcol, brcol, norm_g_l, wdn, wout, x2d, mods, seq, tm=256):
    rows, d = x2d.shape
    rspec = lambda cb: pl.BlockSpec((tm, d), lambda i: (i, cb))
    ospec = lambda dr: pl.BlockSpec((1, tm, d), lambda i: (dr, i, 0))
    wspec = pl.BlockSpec((d, d), lambda i: (0, 0), pipeline_mode=pl.Buffered(1))
    return pl.pallas_call(
        _merge_kernel,
        grid=(rows // tm,),
        in_specs=[rspec(0), ospec(0), ospec(1), rspec(zcol), rspec(brcol), rspec(brcol + 1),
                  pl.BlockSpec((1, d), lambda i: (0, 0)), wspec, wspec, rspec(0),
                  pl.BlockSpec((1, 1, d), lambda i: ((i * tm) // seq, 0, 2))],
        out_specs=rspec(0),
        out_shape=jax.ShapeDtypeStruct((rows, d), F32),
        compiler_params=_params(("arbitrary",)),
        name="merge",
    )(y_conv, o2, o2, p, p, p, norm_g_l, wdn, wout, x2d, mods)


ROUTE_TM = 256


def _top_values(vals, count):
    outs = []
    for _ in range(count):
        m = jnp.max(vals, axis=0, keepdims=True)
        outs.append(m)
        vals = jnp.where(vals == m, NEG_BIG, vals)
    return outs


def _route_kernel(q_ref, k1_ref, k2_ref, s1_ref, s2_ref, p1_ref, p2_ref, tau_ref):
    q = q_ref[...]
    half = PEER_KEYS
    for h in range(PEER_HEADS):
        q1 = q[:, (2 * h) * half:(2 * h + 1) * half]
        q2 = q[:, (2 * h + 1) * half:(2 * h + 2) * half]
        s1 = _dot_nt(k1_ref[h], q1, HI)
        s2 = _dot_nt(k2_ref[h], q2, HI)
        v1 = _top_values(s1, PEER_TOPK)
        v2 = _top_values(s2, PEER_TOPK)
        v2s = jnp.concatenate(v2, axis=0)
        pieces = [v1[a] + v2s[0:PEER_TOPK // (a + 1)] for a in range(PEER_TOPK)]
        n_cand = sum(p.shape[0] for p in pieces)
        pad = -n_cand % 8
        if pad:
            pieces.append(jnp.full((pad, v2s.shape[1]), NEG_BIG, F32))
        cand = jnp.concatenate(pieces, axis=0)
        tau = _top_values(cand, PEER_TOPK)[-1]
        best = v1[0] + v2[0]
        zsum = jnp.sum(jnp.where(cand >= tau, jnp.exp(cand - best), 0.0), axis=0, keepdims=True)
        s1_ref[h] = s1
        s2_ref[h] = s2
        p1_ref[h] = jnp.exp(s1 - v1[0]) / zsum
        p2_ref[h] = jnp.exp(s2 - v2[0])
        tau_ref[h:h + 1, :] = tau


def _route(q2d, key1, key2):
    rows, d = q2d.shape
    tm = ROUTE_TM
    big = jax.ShapeDtypeStruct((PEER_HEADS, PEER_KEYS, rows), F32)
    bspec = pl.BlockSpec((PEER_HEADS, PEER_KEYS, tm), lambda i: (0, 0, i))
    kspec = pl.BlockSpec((PEER_HEADS, PEER_KEYS, PEER_KEYS), lambda i: (0, 0, 0))
    return pl.pallas_call(
        _route_kernel,
        grid=(rows // tm,),
        in_specs=[pl.BlockSpec((tm, d), lambda i: (i, 0)), kspec, kspec],
        out_specs=[bspec, bspec, bspec, bspec, pl.BlockSpec((PEER_HEADS, tm), lambda i: (0, i))],
        out_shape=[big, big, big, big, jax.ShapeDtypeStruct((PEER_HEADS, rows), F32)],
        compiler_params=_params(("arbitrary",)),
        name="peer_route",
    )(q2d, key1, key2)


PEER_TM = 512
PEER_TE = 512
PEER_TC = 256
GELU_C = 0.7978845608028654


def _routed_gelu(x, wd):
    inner = x * (GELU_C + (GELU_C * 0.044715) * (x * x))
    hx = 0.5 * x
    return ((hx + hx * jnp.tanh(inner)) * wd).astype(BF16)


def _experts_simple_kernel(ht_ref, u_ref, vt_ref, s1_ref, s2_ref, p1_ref, p2_ref, tau_ref, o_ref,
                           acc_ref, g_ref, *, wc):
    e = pl.program_id(1)
    na = PEER_TE // PEER_KEYS

    @pl.when(e == 0)
    def _():
        acc_ref[...] = jnp.zeros_like(acc_ref)

    a_t = _dot(u_ref[...], ht_ref[...])
    row0 = (e % 2) * na
    for al in range(na):
        rows = slice(al * PEER_KEYS, (al + 1) * PEER_KEYS)
        for tc in range(PEER_TM // wc):
            ts = slice(tc * wc, (tc + 1) * wc)
            wd = jnp.zeros((PEER_KEYS, wc), F32)
            for h in range(PEER_HEADS):
                s1r = s1_ref[h, pl.ds(row0 + al, 1), ts]
                p1r = p1_ref[h, pl.ds(row0 + al, 1), ts]
                score = s1r + s2_ref[h, :, ts]
                wd = wd + jnp.where(score >= tau_ref[h:h + 1, ts], p1r * p2_ref[h, :, ts], 0.0)
            g_ref[rows, ts] = _routed_gelu(a_t[rows, ts], wd)
    acc_ref[...] += _dot(vt_ref[...], g_ref[...])

    @pl.when(e == pl.num_programs(1) - 1)
    def _():
        o_ref[...] = acc_ref[...]


def _experts_kernel(ht_ref, u_ref, vt_ref, s1_ref, s2_ref, p1_ref, p2_ref, tau_ref, o_ref,
                    acc_ref, a0_ref, a1_ref, g0_ref, g1_ref, *, wc, am, bsplit, fine):
    t = pl.program_id(1)
    na = PEER_TE // PEER_KEYS

    @pl.when(t == 0)
    def _():
        acc_ref[...] = jnp.zeros_like(acc_ref)
        for ref in (a0_ref, a1_ref, g0_ref, g1_ref):
            ref[...] = jnp.zeros_like(ref)

    ntc = PEER_TM // PEER_TC
    d_model = acc_ref.shape[0]
    mb = d_model // bsplit

    def tick(a_cur, a_prev, g_cur, g_prev, row0):
        def a_piece(m, n):
            rows = slice(m * am, (m + 1) * am)
            ts = slice(n * PEER_TC, (n + 1) * PEER_TC)
            a_cur[rows, ts] = _dot(u_ref[rows, :], ht_ref[:, ts])

        def b_piece(m):
            rows = slice(m * mb, (m + 1) * mb)
            acc_ref[rows, :] += _dot(vt_ref[rows, :], g_cur[...])

        def w_stage():
            for al in range(na):
                rows = slice(al * PEER_KEYS, (al + 1) * PEER_KEYS)
                for tc in range(PEER_TM // wc):
                    ts = slice(tc * wc, (tc + 1) * wc)
                    wd = jnp.zeros((PEER_KEYS, wc), F32)
                    for h in range(PEER_HEADS):
                        s1r = s1_ref[h, row0 + al:row0 + al + 1, ts]
                        p1r = p1_ref[h, row0 + al:row0 + al + 1, ts]
                        score = s1r + s2_ref[h, :, ts]
                        wd = wd + jnp.where(score >= tau_ref[h:h + 1, ts],
                                            p1r * p2_ref[h, :, ts], 0.0)
                        if h < PEER_HEADS - 1:
                            yield
                    g_prev[rows, ts] = _routed_gelu(a_prev[rows, ts], wd)
                    yield

        mxu = [functools.partial(a_piece, m, n) for m in range(PEER_TE // am)
               for n in range(ntc)]
        mxu += [functools.partial(b_piece, m) for m in range(bsplit)]
        n_w = na * (PEER_TM // wc) * PEER_HEADS
        per = -(-n_w // len(mxu)) if fine else 0
        w_steps = w_stage()
        for piece in mxu:
            piece()
            for _ in range(per):
                next(w_steps, None)
        for _ in w_steps:
            pass

    @pl.when(t % 2 == 0)
    def _():
        tick(a0_ref, a1_ref, g0_ref, g1_ref, na)

    @pl.when(t % 2 == 1)
    def _():
        tick(a1_ref, a0_ref, g1_ref, g0_ref, 0)

    @pl.when(t == pl.num_programs(1) - 1)
    def _():
        o_ref[...] = acc_ref[...]


def _experts(h_t, u_bf16, vt_bf16, s1, s2, p1, p2, tau, tile0, ntiles, cfg, name):
    d, rows = h_t.shape
    ne = u_bf16.shape[0]
    tm, te = PEER_TM, PEER_TE
    nblk = ne // te
    na = te // PEER_KEYS
    assert 2 * na == 8 and nblk % 2 == 0
    tok = lambda i: tile0 + i
    bspec = pl.BlockSpec((PEER_HEADS, PEER_KEYS, tm), lambda i, t: (0, 0, tok(i)))
    common = dict(
        out_specs=pl.BlockSpec((d, tm), lambda i, t: (0, i)),
        out_shape=jax.ShapeDtypeStruct((d, ntiles * tm), F32),
        compiler_params=_params(("arbitrary", "arbitrary")),
        name=name,
    )
    args = (h_t, u_bf16, vt_bf16, s1, s2, p1, p2, tau)
    if not cfg["pipelined"]:
        aspec = pl.BlockSpec((PEER_HEADS, 2 * na, tm), lambda i, t: (0, t // 2, tok(i)))
        return pl.pallas_call(
            functools.partial(_experts_simple_kernel, wc=cfg["wc"]),
            grid=(ntiles, nblk),
            in_specs=[pl.BlockSpec((d, tm), lambda i, t: (0, tok(i))),
                      pl.BlockSpec((te, d), lambda i, t: (t, 0)),
                      pl.BlockSpec((d, te), lambda i, t: (0, t)),
                      aspec, bspec, aspec, bspec,
                      pl.BlockSpec((PEER_HEADS, tm), lambda i, t: (0, tok(i)))],
            scratch_shapes=[pltpu.VMEM((d, tm), F32), pltpu.VMEM((te, tm), BF16)],
            **common)(*args)
    aspec = pl.BlockSpec((PEER_HEADS, 2 * na, tm),
                         lambda i, t: (0, jnp.clip(t - 1, 0, nblk - 1) // 2, tok(i)))
    return pl.pallas_call(
        functools.partial(_experts_kernel, wc=cfg["wc"], am=cfg["am"], bsplit=cfg["bsplit"],
                          fine=cfg["fine"]),
        grid=(ntiles, nblk + 2),
        in_specs=[pl.BlockSpec((d, tm), lambda i, t: (0, tok(i))),
                  pl.BlockSpec((te, d), lambda i, t: (jnp.minimum(t, nblk - 1), 0)),
                  pl.BlockSpec((d, te), lambda i, t: (0, jnp.clip(t - 2, 0, nblk - 1))),
                  aspec, bspec, aspec, bspec,
                  pl.BlockSpec((PEER_HEADS, tm), lambda i, t: (0, tok(i)))],
        scratch_shapes=[pltpu.VMEM((d, tm), F32), pltpu.VMEM((te, tm), F32),
                        pltpu.VMEM((te, tm), F32), pltpu.VMEM((te, tm), BF16),
                        pltpu.VMEM((te, tm), BF16)],
        **common)(*args)


def _final_kernel(x_ref, yt_ref, m5_ref, g_ref, o_ref):
    x = x_ref[...] + m5_ref[0] * yt_ref[...].T
    ms = jnp.mean(x * x, axis=-1, keepdims=True)
    o_ref[...] = x * lax.rsqrt(ms + EPS) * g_ref[...]


def _final(x1, y_t, mods, final_g, seq, tm=512):
    rows, d = x1.shape
    rspec = pl.BlockSpec((tm, d), lambda i: (i, 0))
    return pl.pallas_call(
        _final_kernel,
        grid=(rows // tm,),
        in_specs=[rspec, pl.BlockSpec((d, tm), lambda i: (0, i)),
                  pl.BlockSpec((1, 1, d), lambda i: ((i * tm) // seq, 0, 5)),
                  pl.BlockSpec((1, d), lambda i: (0, 0))],
        out_specs=rspec,
        out_shape=jax.ShapeDtypeStruct((rows, d), F32),
        compiler_params=_params(("arbitrary",)),
        name="final",
    )(x1, y_t, mods, final_g.reshape(1, d))


def kernel(x, c, ctx, c_ctx, w_mod, b_mod, norm1_g, norm2_g, w_in, conv_w, conv_b, conv_ln_g,
           conv_ln_b, w_conv_out, dn_short_w, dn_a_log, dn_dt_bias, dn_norm_g, w_dn_out, w_out,
           peer_w_q, peer_key1, peer_key2, peer_u, peer_v, final_g):
    batch, seq, d = x.shape
    lc = ctx.shape[1]
    assert w_mod.shape[0] == 1, "single-layer block"
    cdim = conv_w.shape[2]
    nqk = DN_HEADS * DN_DK
    col_glu = 2 * cdim
    col_qkv = 3 * nqk
    col_ab = 4 * DN_HEADS
    s_qkv = col_glu
    s_z = s_qkv + col_qkv
    s_ab = s_z + nqk
    s_br = s_ab + col_ab

    c8 = jnp.zeros((8, d), F32).at[:batch].set(c).at[batch].set(c_ctx)
    mods = _mod(c8, w_mod[0], b_mod[0]).reshape(8, 1, N_MOD * d)

    w_in0 = w_in[0]
    w_main = jnp.concatenate([w_in0[:, :s_ab], w_in0[:, s_br:]], axis=1).astype(BF16)
    w_ab = jnp.pad(w_in0[:, s_ab:s_br], ((0, 0), (0, LANES - col_ab)))

    x2d = x.reshape(batch * seq, d)
    p, ab_l = _norm_proj(x2d, norm1_g[0], mods, 0, 1, seq, 0, w_main, w_ab=w_ab, name="in_proj")
    pc, ab_c = _norm_proj(ctx.reshape(batch * lc, d), norm1_g[0], mods, 0, 1, batch * lc, batch,
                          w_main, w_ab=w_ab, cols=(s_qkv, col_qkv), name="in_proj_ctx")

    y_conv = _conformer(p, conv_w[0], conv_b[0], conv_ln_g[0], conv_ln_b[0],
                        w_conv_out[0].astype(BF16), cdim)

    short_w8 = jnp.pad(dn_short_w[0], ((0, 8 - SHORT_CONV), (0, 0)))
    ql, kl, vl = _dn_prep(p, s_qkv // LANES, short_w8, batch, seq)
    qc, kc, vc = _dn_prep(pc, 0, short_w8, batch, lc)
    gate_lanes = lambda t: jnp.pad(
        jnp.pad(t, ((0, 0), (0, DN_HEADS))).reshape(1, col_ab), ((0, 0), (0, LANES - col_ab)))
    o_l = _dn_scan(qc, kc, vc, ab_c.reshape(batch, lc, LANES), ql, kl, vl,
                   ab_l.reshape(batch, seq, LANES), gate_lanes(dn_a_log[0]),
                   gate_lanes(dn_dt_bias[0]))

    zcol = s_z // d
    brcol = (s_br - col_ab) // d
    x1 = _merge(y_conv, o_l.reshape(2, batch * seq, nqk), p, zcol, brcol,
                jnp.tile(dn_norm_g[0], DN_HEADS).reshape(1, nqk), w_dn_out[0].astype(BF16),
                w_out[0].astype(BF16), x2d, mods, seq)

    q2d, h2_t = _norm_proj(x1, norm2_g[0], mods, 3, 4, seq, 0, peer_w_q[0].astype(BF16),
                           with_h=True, name="peer_q")
    s1, s2, p1, p2, tau = _route(q2d, peer_key1[0], peer_key2[0])
    cfgs = [("peer_experts_simple", dict(pipelined=False, wc=256)),
            ("peer_experts_fine", dict(pipelined=True, wc=256, am=128, bsplit=8, fine=True)),
            ("peer_experts_coarse", dict(pipelined=True, wc=128, am=512, bsplit=1, fine=False)),
            ("peer_experts_fine128", dict(pipelined=True, wc=128, am=128, bsplit=8, fine=True))]
    ntiles = (batch * seq) // PEER_TM // len(cfgs)
    u16 = peer_u[0].astype(BF16)
    vt16 = peer_v[0].T.astype(BF16)
    y_t = jnp.concatenate(
        [_experts(h2_t, u16, vt16, s1, s2, p1, p2, tau, k * ntiles, ntiles, cfg, name)
         for k, (name, cfg) in enumerate(cfgs)], axis=1)
    out = _final(x1, y_t, mods, final_g, seq)
    return out.reshape(batch, seq, d)
```

```python
import functools

import jax
import jax.numpy as jnp
from jax import lax
from jax.experimental import pallas as pl
from jax.experimental.pallas import tpu as pltpu

F32 = jnp.float32
BF16 = jnp.bfloat16
HI = lax.Precision.HIGHEST

EPS = 1e-6
N_MOD = 6
GRID_W = 64
CONV_WIDTH = 31
DN_HEADS = 16
DN_DK = 128
SHORT_CONV = 5
PEER_HEADS = 8
PEER_KEYS = 128
PEER_TOPK = 16

LANES = 128
DN_CHUNK = 128
NEG_BIG = -3.0e38
VMEM_LIMIT = 56 << 20


def _params(sem):
    return pltpu.CompilerParams(dimension_semantics=sem, vmem_limit_bytes=VMEM_LIMIT)


def _sigmoid(x):
    return 1.0 / (1.0 + jnp.exp(-x))


def _silu(x):
    return x * _sigmoid(x)


def _dot(a, b, precision=None):
    return jnp.dot(a, b, preferred_element_type=F32, precision=precision)


def _dot_nt(a, b, precision=None):
    return lax.dot_general(a, b, (((1,), (1,)), ((), ())), preferred_element_type=F32,
                           precision=precision)


def _mod_kernel(c_ref, w_ref, b_ref, o_ref):
    c = c_ref[...]
    o_ref[...] = _dot(_silu(c), w_ref[...], HI) + b_ref[...]


def _mod(c8, w_mod, b_mod):
    d, n = w_mod.shape
    tn = 1024
    return pl.pallas_call(
        _mod_kernel,
        grid=(n // tn,),
        in_specs=[pl.BlockSpec((8, d), lambda j: (0, 0)),
                  pl.BlockSpec((d, tn), lambda j: (0, j)),
                  pl.BlockSpec((1, tn), lambda j: (0, j))],
        out_specs=pl.BlockSpec((8, tn), lambda j: (0, j)),
        out_shape=jax.ShapeDtypeStruct((8, n), F32),
        compiler_params=_params(("arbitrary",)),
        name="mod",
    )(c8, w_mod, b_mod.reshape(1, n))


def _norm_proj_kernel(x_ref, g_ref, sh_ref, sc_ref, w_ref, *rest, with_ab, with_h):
    rest = list(rest)
    wab_ref = rest.pop(0) if with_ab else None
    o_ref = rest.pop(0)
    oab_ref = rest.pop(0) if with_ab else None
    oh_ref = rest.pop(0) if with_h else None
    h_ref = rest.pop(0)

    @pl.when(pl.program_id(1) == 0)
    def _():
        x = x_ref[...]
        ms = jnp.mean(x * x, axis=-1, keepdims=True)
        y = x * lax.rsqrt(ms + EPS) * g_ref[...]
        h = y * (1.0 + sc_ref[0]) + sh_ref[0]
        hb = h.astype(BF16)
        h_ref[...] = hb
        if with_ab:
            oab_ref[...] = _dot(h, wab_ref[...], HI)
        if with_h:
            oh_ref[...] = h.T.astype(BF16)

    o_ref[...] = _dot(h_ref[...], w_ref[...])


def _norm_proj(x2d, g, mods, shift_chunk, scale_chunk, rows_per_mod, mod_row0, w_bf16, w_ab=None,
               with_h=False, cols=None, tn=512, name="norm_proj"):
    rows, d = x2d.shape
    col0, n = cols if cols is not None else (0, w_bf16.shape[1])
    with_ab = w_ab is not None
    tm = min(1024, rows_per_mod)
    assert rows % tm == 0 and rows_per_mod % tm == 0 and n % tn == 0 and col0 % tn == 0
    jb0 = col0 // tn

    def batch_of_tile(i):
        return mod_row0 + (i * tm) // rows_per_mod

    in_specs = [pl.BlockSpec((tm, d), lambda i, j: (i, 0)),
                pl.BlockSpec((1, d), lambda i, j: (0, 0)),
                pl.BlockSpec((1, 1, d), lambda i, j: (batch_of_tile(i), 0, shift_chunk)),
                pl.BlockSpec((1, 1, d), lambda i, j: (batch_of_tile(i), 0, scale_chunk)),
                pl.BlockSpec((d, tn), lambda i, j: (0, jb0 + j))]
    args = [x2d, g.reshape(1, d), mods, mods, w_bf16]
    out_specs = [pl.BlockSpec((tm, tn), lambda i, j: (i, j))]
    out_shape = [jax.ShapeDtypeStruct((rows, n), F32)]
    if with_ab:
        in_specs.append(pl.BlockSpec((d, LANES), lambda i, j: (0, 0)))
        args.append(w_ab)
        out_specs.append(pl.BlockSpec((tm, LANES), lambda i, j: (i, 0)))
        out_shape.append(jax.ShapeDtypeStruct((rows, LANES), F32))
    if with_h:
        out_specs.append(pl.BlockSpec((d, tm), lambda i, j: (0, i)))
        out_shape.append(jax.ShapeDtypeStruct((d, rows), BF16))
    return pl.pallas_call(
        functools.partial(_norm_proj_kernel, with_ab=with_ab, with_h=with_h),
        grid=(rows // tm, n // tn),
        in_specs=in_specs,
        out_specs=out_specs,
        out_shape=out_shape,
        scratch_shapes=[pltpu.VMEM((tm, d), BF16)],
        compiler_params=_params(("arbitrary", "arbitrary")),
        name=name,
    )(*args)


CONV_ROWS = 4
CONV_CW = 256
CONV_PAD = 16


def _conv_kernel(a_ref, gt_ref, cw_ref, cb_ref, lg_ref, lb_ref, w_ref, o_ref, pad_ref, y_ref,
                 shift_ref):
    c = a_ref.shape[1]
    ncc = c // CONV_CW
    u = a_ref[...] * _sigmoid(gt_ref[...])
    zero = jnp.zeros((CONV_PAD, CONV_CW), F32)
    for r in range(CONV_ROWS):
        for cc in range(ncc):
            pad_ref[r, cc, 0:CONV_PAD, :] = zero
            pad_ref[r, cc, CONV_PAD:CONV_PAD + GRID_W, :] = (
                u[r * GRID_W:(r + 1) * GRID_W, cc * CONV_CW:(cc + 1) * CONV_CW])
            pad_ref[r, cc, CONV_PAD + GRID_W:2 * CONV_PAD + GRID_W, :] = zero

    half = CONV_WIDTH // 2

    sub = 8
    span = GRID_W + 2 * CONV_PAD - sub

    def body(idx, carry):
        r = idx // ncc
        cc = idx % ncc
        for s in range(1, sub):
            shift_ref[s - 1] = pad_ref[r, cc, s:s + span, :]
        acc = jnp.zeros((GRID_W, CONV_CW), F32)
        for k in range(CONV_WIDTH):
            start = CONV_PAD - half + k
            s = start % sub
            base = start - s
            if s == 0:
                src = pad_ref[r, cc, base:base + GRID_W, :]
            else:
                src = shift_ref[s - 1, base:base + GRID_W, :]
            acc = acc + src * cw_ref[cc, k:k + 1, :]
        y_ref[cc, pl.ds(pl.multiple_of(r * GRID_W, GRID_W), GRID_W), :] = acc
        return carry

    lax.fori_loop(0, CONV_ROWS * ncc, body, 0)

    y = jnp.concatenate([y_ref[cc] for cc in range(ncc)], axis=-1) + cb_ref[...]
    mu = jnp.mean(y, axis=-1, keepdims=True)
    yc = y - mu
    var = jnp.mean(yc * yc, axis=-1, keepdims=True)
    yn = yc * lax.rsqrt(var + EPS) * lg_ref[...] + lb_ref[...]
    o_ref[...] = _dot(_silu(yn).astype(BF16), w_ref[...])


def _conformer(p, conv_w, conv_b, ln_g, ln_b, w_out_bf16, c):
    rows = p.shape[0]
    tm = CONV_ROWS * GRID_W
    ncc = c // CONV_CW
    cw = jnp.pad(conv_w, ((0, 32 - CONV_WIDTH), (0, 0)))
    cw = cw.reshape(32, ncc, CONV_CW).transpose(1, 0, 2)
    return pl.pallas_call(
        _conv_kernel,
        grid=(rows // tm,),
        in_specs=[pl.BlockSpec((tm, c), lambda i: (i, 0)),
                  pl.BlockSpec((tm, c), lambda i: (i, 1)),
                  pl.BlockSpec((ncc, 32, CONV_CW), lambda i: (0, 0, 0)),
                  pl.BlockSpec((1, c), lambda i: (0, 0)),
                  pl.BlockSpec((1, c), lambda i: (0, 0)),
                  pl.BlockSpec((1, c), lambda i: (0, 0)),
                  pl.BlockSpec((c, c), lambda i: (0, 0))],
        out_specs=pl.BlockSpec((tm, c), lambda i: (i, 0)),
        out_shape=jax.ShapeDtypeStruct((rows, c), F32),
        scratch_shapes=[pltpu.VMEM((CONV_ROWS, ncc, GRID_W + 2 * CONV_PAD, CONV_CW), F32),
                        pltpu.VMEM((ncc, tm, CONV_CW), F32),
                        pltpu.VMEM((7, GRID_W + 2 * CONV_PAD - 8, CONV_CW), F32)],
        compiler_params=_params(("arbitrary",)),
        name="conformer",
    )(p, p, cw, conv_b.reshape(1, c), ln_g.reshape(1, c), ln_b.reshape(1, c), w_out_bf16)


DN_PAD = 8
DN_RC = 256


def _dnprep_kernel(q_ref, k_ref, v_ref, wq_ref, wk_ref, wv_ref, qo_ref, ko_ref, vo_ref, pad_ref):
    length = q_ref.shape[0]
    zero = jnp.zeros((DN_PAD, LANES), F32)
    half = SHORT_CONV // 2

    def run(x_ref, w_ref, o_ref, mode):
        pad_ref[0:DN_PAD, :] = zero
        pad_ref[DN_PAD + length:2 * DN_PAD + length, :] = zero
        pad_ref[DN_PAD:DN_PAD + length, :] = x_ref[...]
        for r0 in range(0, length, DN_RC):
            acc = jnp.zeros((DN_RC, LANES), F32)
            for k in range(SHORT_CONV):
                start = DN_PAD - half + k + r0
                acc = acc + pad_ref[start:start + DN_RC, :] * w_ref[k:k + 1, :]
            y = _silu(acc)
            if mode != "v":
                y = y * lax.rsqrt(jnp.sum(y * y, axis=-1, keepdims=True) + EPS)
            if mode == "q":
                y = y * (DN_DK ** -0.5)
            o_ref[0, r0:r0 + DN_RC, :] = y.astype(o_ref.dtype)

    run(q_ref, wq_ref, qo_ref, "q")
    run(k_ref, wk_ref, ko_ref, "k")
    run(v_ref, wv_ref, vo_ref, "v")


def _dn_prep(p, col0, short_w8, batch, length):
    nh = DN_HEADS
    out = jax.ShapeDtypeStruct((batch, length, nh * LANES), BF16)
    ospec = pl.BlockSpec((1, length, LANES), lambda b, h: (b, 0, h))
    return pl.pallas_call(
        _dnprep_kernel,
        grid=(batch, nh),
        in_specs=[pl.BlockSpec((length, LANES), lambda b, h: (b, col0 + h)),
                  pl.BlockSpec((length, LANES), lambda b, h: (b, col0 + nh + h)),
                  pl.BlockSpec((length, LANES), lambda b, h: (b, col0 + 2 * nh + h)),
                  pl.BlockSpec((8, LANES), lambda b, h: (0, h)),
                  pl.BlockSpec((8, LANES), lambda b, h: (0, nh + h)),
                  pl.BlockSpec((8, LANES), lambda b, h: (0, 2 * nh + h))],
        out_specs=[ospec, ospec, ospec],
        out_shape=[out, out, out],
        scratch_shapes=[pltpu.VMEM((length + 2 * DN_PAD, LANES), F32)],
        compiler_params=_params(("arbitrary", "arbitrary")),
        name="dn_prep",
    )(p, p, p, short_w8, short_w8, short_w8)


DN_HB = 8
DN_BASE = 8


def _scan_kernel(qc_ref, kc_ref, vc_ref, abc_ref, ql_ref, kl_ref, vl_ref, abl_ref,
                 alog_ref, dtb_ref, o_ref, s_ref):
    ch = DN_CHUNK
    hb = DN_HB
    hg = pl.program_id(1)
    d = pl.program_id(2)
    fwd = d == 0

    row = lax.broadcasted_iota(jnp.int32, (ch, ch), 0)
    col = lax.broadcasted_iota(jnp.int32, (ch, ch), 1)
    diff = jnp.where(fwd, row - col, col - row)
    incl = diff >= 0
    strict = diff > 0
    tri = incl.astype(F32)

    def same_block(size):
        shift = size.bit_length() - 1
        return (row >> shift) == (col >> shift)

    base_mask = same_block(DN_BASE) & strict
    level_masks = []
    size = DN_BASE
    while size < ch:
        level_masks.append(same_block(2 * size) & jnp.logical_not(same_block(size)) & strict)
        size *= 2

    lane0 = 2 * DN_HEADS * d + hg * hb

    neg_a = -jnp.exp(alog_ref[...])
    dtb = dtb_ref[...]

    s_ref[...] = jnp.zeros_like(s_ref)

    heads = range(hb)

    def each(fn, *lists):
        return [fn(*args) for args in zip(*lists)]

    def inverse_minus_identity(ms):
        m0 = each(lambda m: jnp.where(base_mask, m, 0.0), ms)
        m016 = each(lambda t: t.astype(BF16), m0)
        a = each(lambda t: _dot(t, t), m016)
        a16 = each(lambda t: t.astype(BF16), a)
        b = each(lambda t: _dot(t, t), a16)
        f = each(lambda a_, m0_, m16, a16_: a_ - m0_ - _dot(m16, a16_), a, m0, m016, a16)
        e = each(lambda f_, b_: f_ + b_ + _dot(f_.astype(BF16), b_.astype(BF16)), f, b)
        for mask in level_masks:
            c = each(lambda m: jnp.where(mask, m, 0.0), ms)
            tc = each(lambda c_, e_: c_ + _dot(e_.astype(BF16), c_.astype(BF16)), c, e)
            e = each(lambda e_, tc_: e_ - tc_ - _dot(tc_.astype(BF16), e_.astype(BF16)), e, tc)
        return e

    def phase(q_ref, k_ref, v_ref, ab_ref, n_chunks, write_out):
        def body(p, carry):
            ci = jnp.where(fwd, p, n_chunks - 1 - p)
            r0 = pl.multiple_of(ci * ch, ch)
            ab = ab_ref[0, pl.ds(r0, ch), :]
            x = ab + dtb
            softplus = jnp.maximum(x, 0.0) + jnp.log(1.0 + jnp.exp(-jnp.abs(x)))
            g_all = neg_a * softplus
            b_all = _sigmoid(ab)
            gcum_all = _dot(tri, g_all, HI)
            gcum_t = gcum_all.T
            gtot_all = jnp.sum(g_all, axis=0, keepdims=True)
            q16 = [q_ref[0, pl.ds(r0, ch), c * ch:(c + 1) * ch] for c in heads]
            k16 = [k_ref[0, pl.ds(r0, ch), c * ch:(c + 1) * ch] for c in heads]
            q = each(lambda t: t.astype(F32), q16)
            k = each(lambda t: t.astype(F32), k16)
            v = [v_ref[0, pl.ds(r0, ch), c * ch:(c + 1) * ch].astype(F32) for c in heads]
            s = [s_ref[c] for c in heads]
            la = [lane0 + c for c in heads]
            gc_col = [jnp.sum(jnp.where(col == l, gcum_all, 0.0), axis=1, keepdims=True)
                      for l in la]
            beta_col = [jnp.sum(jnp.where(col == l + DN_HEADS, b_all, 0.0), axis=1, keepdims=True)
                        for l in la]
            gc_row = [jnp.sum(jnp.where(row == l, gcum_t, 0.0), axis=0, keepdims=True) for l in la]
            gt = [jnp.sum(jnp.where(col[0:1] == l, gtot_all, 0.0), axis=1, keepdims=True)
                  for l in la]
            decay = each(lambda gc, gr: jnp.where(incl, jnp.exp(gc - gr), 0.0), gc_col, gc_row)
            eg = each(jnp.exp, gc_col)
            if write_out:
                qk = each(lambda q_, k_: _dot_nt(jnp.concatenate([q_, k_], axis=0), k_), q16, k16)
                attn = each(lambda t, dc: t[:ch] * dc, qk, decay)
                kk = each(lambda t: t[ch:], qk)
            else:
                kk = each(lambda k_: _dot_nt(k_, k_), k16)
            m = each(lambda kk_, b_, dc: jnp.where(strict, kk_ * b_ * dc, 0.0), kk, beta_col, decay)
            e = inverse_minus_identity(m)
            rhs = each(lambda v_, k_, b_, eg_: jnp.concatenate([v_ * b_, k_ * (b_ * eg_)], axis=1),
                       v, k, beta_col, eg)
            uw = each(lambda r, e_: r + _dot(e_.astype(BF16), r.astype(BF16)), rhs, e)
            u = each(lambda t: t[:, :LANES], uw)
            w = each(lambda t: t[:, LANES:], uw)
            s16 = each(lambda t: t.astype(BF16), s)
            kdt = each(lambda k_, gt_, gr: k_.T * jnp.exp(gt_ - gr), k, gt, gc_row)
            if write_out:
                ws = each(lambda w_, q_, eg_, s_: _dot(
                    jnp.concatenate([w_, q_ * eg_], axis=0).astype(BF16), s_), w, q, eg, s16)
                v_new = each(lambda u_, ws_: u_ - ws_[:ch], u, ws)
                r2 = each(lambda a_, kd, vn: _dot(
                    jnp.concatenate([a_, kd], axis=0).astype(BF16), vn.astype(BF16)),
                    attn, kdt, v_new)
                o = each(lambda ws_, r2_: ws_[ch:] + r2_[:ch], ws, r2)
                ds = each(lambda t: t[ch:], r2)
            else:
                v_new = each(lambda u_, w_, s_: u_ - _dot(w_.astype(BF16), s_), u, w, s16)
                ds = each(lambda kd, vn: _dot(kd.astype(BF16), vn.astype(BF16)), kdt, v_new)
            s_new = each(lambda s_, gt_, ds_: s_ * jnp.exp(gt_) + ds_, s, gt, ds)
            for c in heads:
                if write_out:
                    o_ref[0, 0, pl.ds(r0, ch), c * ch:(c + 1) * ch] = o[c]
                s_ref[c] = s_new[c]
            return carry

        lax.fori_loop(0, n_chunks, body, 0)

    phase(qc_ref, kc_ref, vc_ref, abc_ref, qc_ref.shape[1] // ch, False)
    phase(ql_ref, kl_ref, vl_ref, abl_ref, ql_ref.shape[1] // ch, True)


def _dn_scan(qc, kc, vc, abc, ql, kl, vl, abl, alog_l, dtb_l):
    batch, lc, hv = qc.shape
    ll = ql.shape[1]
    hb = DN_HB
    w = hb * LANES
    cspec = pl.BlockSpec((1, lc, w), lambda b, g, d: (b, 0, g))
    lspec = pl.BlockSpec((1, ll, w), lambda b, g, d: (b, 0, g))
    return pl.pallas_call(
        _scan_kernel,
        grid=(batch, DN_HEADS // hb, 2),
        in_specs=[cspec, cspec, cspec,
                  pl.BlockSpec((1, lc, LANES), lambda b, g, d: (b, 0, 0)),
                  lspec, lspec, lspec,
                  pl.BlockSpec((1, ll, LANES), lambda b, g, d: (b, 0, 0)),
                  pl.BlockSpec((1, LANES), lambda b, g, d: (0, 0)),
                  pl.BlockSpec((1, LANES), lambda b, g, d: (0, 0))],
        out_specs=pl.BlockSpec((1, 1, ll, w), lambda b, g, d: (d, b, 0, g)),
        out_shape=jax.ShapeDtypeStruct((2, batch, ll, hv), F32),
        scratch_shapes=[pltpu.VMEM((hb, DN_DK, LANES), F32)],
        compiler_params=_params(("arbitrary", "arbitrary", "arbitrary")),
        name="dn_scan",
    )(qc, kc, vc, abc, ql, kl, vl, abl, alog_l, dtb_l)


def _merge_kernel(yc_ref, of_ref, ob_ref, z_ref, brc_ref, brd_ref, g_ref, wdn_ref, wout_ref,
                  x_ref, m2_ref, out_ref):
    o = of_ref[0] + ob_ref[0]
    parts = []
    for h in range(DN_HEADS):
        oh = o[:, h * LANES:(h + 1) * LANES]
        ms = jnp.mean(oh * oh, axis=-1, keepdims=True)
        parts.append(oh * lax.rsqrt(ms + EPS))
    on = jnp.concatenate(parts, axis=-1) * g_ref[...]
    gated = on * _silu(z_ref[...])
    y_dn = _dot(gated.astype(BF16), wdn_ref[...])
    mix = _sigmoid(brc_ref[...]) * yc_ref[...] + _sigmoid(brd_ref[...]) * y_dn
    y = _dot(mix.astype(BF16), wout_ref[...])
    out_ref[...] = x_ref[...] + m2_ref[0] * y


def _merge(y_conv, o2, p, zcol, brcol, norm_g_l, wdn, wout, x2d, mods, seq, tm=256):
    rows, d = x2d.shape
    rspec = lambda cb: pl.BlockSpec((tm, d), lambda i: (i, cb))
    ospec = lambda dr: pl.BlockSpec((1, tm, d), lambda i: (dr, i, 0))
    wspec = pl.BlockSpec((d, d), lambda i: (0, 0), pipeline_mode=pl.Buffered(1))
    return pl.pallas_call(
        _merge_kernel,
        grid=(rows // tm,),
        in_specs=[rspec(0), ospec(0), ospec(1), rspec(zcol), rspec(brcol), rspec(brcol + 1),
                  pl.BlockSpec((1, d), lambda i: (0, 0)), wspec, wspec, rspec(0),
                  pl.BlockSpec((1, 1, d), lambda i: ((i * tm) // seq, 0, 2))],
        out_specs=rspec(0),
        out_shape=jax.ShapeDtypeStruct((rows, d), F32),
        compiler_params=_params(("arbitrary",)),
        name="merge",
    )(y_conv, o2, o2, p, p, p, norm_g_l, wdn, wout, x2d, mods)


ROUTE_TM = 256


def _top_values(vals, count):
    outs = []
    for _ in range(count):
        m = jnp.max(vals, axis=0, keepdims=True)
        outs.append(m)
        vals = jnp.where(vals == m, NEG_BIG, vals)
    return outs


def _route_kernel(q_ref, k1_ref, k2_ref, thr_ref, s2_ref, p1_ref, p2_ref):
    q = q_ref[...]
    half = PEER_KEYS
    n = PEER_TOPK + 1
    for h in range(PEER_HEADS):
        q1 = q[:, (2 * h) * half:(2 * h + 1) * half]
        q2 = q[:, (2 * h + 1) * half:(2 * h + 2) * half]
        s1 = _dot_nt(k1_ref[h], q1, HI)
        s2 = _dot_nt(k2_ref[h], q2, HI)
        v1 = _top_values(s1, n)
        v2 = _top_values(s2, n)
        v2s = jnp.concatenate(v2, axis=0)
        counts = [n // (a + 1) for a in range(n)]
        pad = -sum(counts) % 8
        tokens = v2s.shape[1]
        second = jnp.concatenate([v2s[0:c] for c in counts]
                                 + [jnp.full((pad, tokens), NEG_BIG, F32)], axis=0)
        cand = jnp.concatenate([v1[a] + v2s[0:c] for a, c in enumerate(counts)]
                               + [jnp.full((pad, tokens), NEG_BIG, F32)], axis=0)
        tops = _top_values(cand, n)
        cut = 0.5 * (tops[PEER_TOPK - 1] + tops[PEER_TOPK])
        need = jnp.concatenate([jnp.broadcast_to(cut - v1[a], (c, tokens))
                                for a, c in enumerate(counts)]
                               + [jnp.full((pad, tokens), -NEG_BIG, F32)], axis=0)
        best = v1[0] + v2[0]
        zsum = jnp.sum(jnp.where(second >= need, jnp.exp(cand - best), 0.0), axis=0,
                       keepdims=True)
        thr_ref[h] = cut - s1
        s2_ref[h] = s2
        p1_ref[h] = jnp.exp(s1 - v1[0]) / zsum
        p2_ref[h] = jnp.exp(s2 - v2[0])


def _route(q2d, key1, key2):
    rows, d = q2d.shape
    tm = ROUTE_TM
    big = jax.ShapeDtypeStruct((PEER_HEADS, PEER_KEYS, rows), F32)
    bspec = pl.BlockSpec((PEER_HEADS, PEER_KEYS, tm), lambda i: (0, 0, i))
    kspec = pl.BlockSpec((PEER_HEADS, PEER_KEYS, PEER_KEYS), lambda i: (0, 0, 0))
    return pl.pallas_call(
        _route_kernel,
        grid=(rows // tm,),
        in_specs=[pl.BlockSpec((tm, d), lambda i: (i, 0)), kspec, kspec],
        out_specs=[bspec, bspec, bspec, bspec],
        out_shape=[big, big, big, big],
        compiler_params=_params(("arbitrary",)),
        name="peer_route",
    )(q2d, key1, key2)


PEER_TM = 512
PEER_TE = 512
PEER_WC = 256
GELU_C = 0.7978845608028654


def _experts_kernel(ht_ref, u_ref, vt_ref, thr_ref, s2_ref, p1_ref, p2_ref, o_ref, acc_ref, g_ref):
    e = pl.program_id(1)
    na = PEER_TE // PEER_KEYS

    @pl.when(e == 0)
    def _():
        acc_ref[...] = jnp.zeros_like(acc_ref)

    a_t = _dot(u_ref[...], ht_ref[...])
    row0 = (e % 2) * na
    for al in range(na):
        rows = slice(al * PEER_KEYS, (al + 1) * PEER_KEYS)
        for tc in range(PEER_TM // PEER_WC):
            ts = slice(tc * PEER_WC, (tc + 1) * PEER_WC)
            wd = jnp.zeros((PEER_KEYS, PEER_WC), F32)
            for h in range(PEER_HEADS):
                need = thr_ref[h, pl.ds(row0 + al, 1), ts]
                p1r = p1_ref[h, pl.ds(row0 + al, 1), ts]
                wd = wd + jnp.where(s2_ref[h, :, ts] >= need, p1r * p2_ref[h, :, ts], 0.0)
            x = a_t[rows, ts]
            inner = x * (GELU_C + (GELU_C * 0.044715) * (x * x))
            hx = 0.5 * x
            g_ref[rows, ts] = ((hx + hx * jnp.tanh(inner)) * wd).astype(BF16)
    acc_ref[...] += _dot(vt_ref[...], g_ref[...])

    @pl.when(e == pl.num_programs(1) - 1)
    def _():
        o_ref[...] = acc_ref[...]


def _experts(h_t, u_bf16, vt_bf16, thr, s2, p1, p2):
    d, rows = h_t.shape
    ne = u_bf16.shape[0]
    tm, te = PEER_TM, PEER_TE
    na = te // PEER_KEYS
    assert 2 * na == 8 and (ne // te) % 2 == 0
    bspec = pl.BlockSpec((PEER_HEADS, PEER_KEYS, tm), lambda i, e: (0, 0, i))
    aspec = pl.BlockSpec((PEER_HEADS, 2 * na, tm), lambda i, e: (0, e // 2, i))
    return pl.pallas_call(
        _experts_kernel,
        grid=(rows // tm, ne // te),
        in_specs=[pl.BlockSpec((d, tm), lambda i, e: (0, i)),
                  pl.BlockSpec((te, d), lambda i, e: (e, 0)),
                  pl.BlockSpec((d, te), lambda i, e: (0, e)),
                  aspec, bspec, aspec, bspec],
        out_specs=pl.BlockSpec((d, tm), lambda i, e: (0, i)),
        out_shape=jax.ShapeDtypeStruct((d, rows), F32),
        scratch_shapes=[pltpu.VMEM((d, tm), F32), pltpu.VMEM((te, tm), BF16)],
        compiler_params=_params(("arbitrary", "arbitrary")),
        name="peer_experts",
    )(h_t, u_bf16, vt_bf16, thr, s2, p1, p2)


def _tables_kernel(u_ref, v_ref, u16_ref, vt16_ref):
    u16_ref[...] = u_ref[...].astype(BF16)
    vt16_ref[...] = v_ref[...].T.astype(BF16)


def _expert_tables(peer_u, peer_v):
    ne, d = peer_u.shape
    te = PEER_TE
    return pl.pallas_call(
        _tables_kernel,
        grid=(ne // te,),
        in_specs=[pl.BlockSpec((te, d), lambda e: (e, 0)), pl.BlockSpec((te, d), lambda e: (e, 0))],
        out_specs=[pl.BlockSpec((te, d), lambda e: (e, 0)), pl.BlockSpec((d, te), lambda e: (0, e))],
        out_shape=[jax.ShapeDtypeStruct((ne, d), BF16), jax.ShapeDtypeStruct((d, ne), BF16)],
        compiler_params=_params(("arbitrary",)),
        name="peer_tables",
    )(peer_u, peer_v)


def _final_kernel(x_ref, yt_ref, m5_ref, g_ref, o_ref):
    x = x_ref[...] + m5_ref[0] * yt_ref[...].T
    ms = jnp.mean(x * x, axis=-1, keepdims=True)
    o_ref[...] = x * lax.rsqrt(ms + EPS) * g_ref[...]


def _final(x1, y_t, mods, final_g, seq, tm=512):
    rows, d = x1.shape
    rspec = pl.BlockSpec((tm, d), lambda i: (i, 0))
    return pl.pallas_call(
        _final_kernel,
        grid=(rows // tm,),
        in_specs=[rspec, pl.BlockSpec((d, tm), lambda i: (0, i)),
                  pl.BlockSpec((1, 1, d), lambda i: ((i * tm) // seq, 0, 5)),
                  pl.BlockSpec((1, d), lambda i: (0, 0))],
        out_specs=rspec,
        out_shape=jax.ShapeDtypeStruct((rows, d), F32),
        compiler_params=_params(("arbitrary",)),
        name="final",
    )(x1, y_t, mods, final_g.reshape(1, d))


def kernel(x, c, ctx, c_ctx, w_mod, b_mod, norm1_g, norm2_g, w_in, conv_w, conv_b, conv_ln_g,
           conv_ln_b, w_conv_out, dn_short_w, dn_a_log, dn_dt_bias, dn_norm_g, w_dn_out, w_out,
           peer_w_q, peer_key1, peer_key2, peer_u, peer_v, final_g):
    batch, seq, d = x.shape
    lc = ctx.shape[1]
    assert w_mod.shape[0] == 1, "single-layer block"
    cdim = conv_w.shape[2]
    nqk = DN_HEADS * DN_DK
    col_glu = 2 * cdim
    col_qkv = 3 * nqk
    col_ab = 4 * DN_HEADS
    s_qkv = col_glu
    s_z = s_qkv + col_qkv
    s_ab = s_z + nqk
    s_br = s_ab + col_ab

    c8 = jnp.zeros((8, d), F32).at[:batch].set(c).at[batch].set(c_ctx)
    mods = _mod(c8, w_mod[0], b_mod[0]).reshape(8, 1, N_MOD * d)

    w_in0 = w_in[0]
    w_main = jnp.concatenate([w_in0[:, :s_ab], w_in0[:, s_br:]], axis=1).astype(BF16)
    w_ab = jnp.pad(w_in0[:, s_ab:s_br], ((0, 0), (0, LANES - col_ab)))

    x2d = x.reshape(batch * seq, d)
    p, ab_l = _norm_proj(x2d, norm1_g[0], mods, 0, 1, seq, 0, w_main, w_ab=w_ab, name="in_proj")
    pc, ab_c = _norm_proj(ctx.reshape(batch * lc, d), norm1_g[0], mods, 0, 1, batch * lc, batch,
                          w_main, w_ab=w_ab, cols=(s_qkv, col_qkv), name="in_proj_ctx")

    y_conv = _conformer(p, conv_w[0], conv_b[0], conv_ln_g[0], conv_ln_b[0],
                        w_conv_out[0].astype(BF16), cdim)

    short_w8 = jnp.pad(dn_short_w[0], ((0, 8 - SHORT_CONV), (0, 0)))
    ql, kl, vl = _dn_prep(p, s_qkv // LANES, short_w8, batch, seq)
    qc, kc, vc = _dn_prep(pc, 0, short_w8, batch, lc)
    gate_lanes = lambda t: jnp.pad(
        jnp.pad(t, ((0, 0), (0, DN_HEADS))).reshape(1, col_ab), ((0, 0), (0, LANES - col_ab)))
    o_l = _dn_scan(qc, kc, vc, ab_c.reshape(batch, lc, LANES), ql, kl, vl,
                   ab_l.reshape(batch, seq, LANES), gate_lanes(dn_a_log[0]),
                   gate_lanes(dn_dt_bias[0]))

    zcol = s_z // d
    brcol = (s_br - col_ab) // d
    x1 = _merge(y_conv, o_l.reshape(2, batch * seq, nqk), p, zcol, brcol,
                jnp.tile(dn_norm_g[0], DN_HEADS).reshape(1, nqk), w_dn_out[0].astype(BF16),
                w_out[0].astype(BF16), x2d, mods, seq)

    q2d, h2_t = _norm_proj(x1, norm2_g[0], mods, 3, 4, seq, 0, peer_w_q[0].astype(BF16),
                           with_h=True, name="peer_q")
    thr, s2, p1, p2 = _route(q2d, peer_key1[0], peer_key2[0])
    u16, vt16 = _expert_tables(peer_u[0], peer_v[0])
    y_t = _experts(h2_t, u16, vt16, thr, s2, p1, p2)
    out = _final(x1, y_t, mods, final_g, seq)
    return out.reshape(batch, seq, d)
```

```python
import functools

import jax
import jax.numpy as jnp
from jax import lax
from jax.experimental import pallas as pl
from jax.experimental.pallas import tpu as pltpu

F32 = jnp.float32
BF16 = jnp.bfloat16
HI = lax.Precision.HIGHEST

EPS = 1e-6
N_MOD = 6
GRID_W = 64
CONV_WIDTH = 31
DN_HEADS = 16
DN_DK = 128
SHORT_CONV = 5
PEER_HEADS = 8
PEER_KEYS = 128
PEER_TOPK = 16

LANES = 128
DN_CHUNK = 128
NEG_BIG = -3.0e38
VMEM_LIMIT = 56 << 20


def _params(sem):
    return pltpu.CompilerParams(dimension_semantics=sem, vmem_limit_bytes=VMEM_LIMIT)


def _sigmoid(x):
    return 1.0 / (1.0 + jnp.exp(-x))


def _silu(x):
    return x * _sigmoid(x)


def _dot(a, b, precision=None):
    return jnp.dot(a, b, preferred_element_type=F32, precision=precision)


def _dot_nt(a, b, precision=None):
    return lax.dot_general(a, b, (((1,), (1,)), ((), ())), preferred_element_type=F32,
                           precision=precision)


def _mod_kernel(c_ref, w_ref, b_ref, o_ref):
    c = c_ref[...]
    o_ref[...] = _dot(_silu(c), w_ref[...], HI) + b_ref[...]


def _mod(c8, w_mod, b_mod):
    d, n = w_mod.shape
    tn = 1024
    return pl.pallas_call(
        _mod_kernel,
        grid=(n // tn,),
        in_specs=[pl.BlockSpec((8, d), lambda j: (0, 0)),
                  pl.BlockSpec((d, tn), lambda j: (0, j)),
                  pl.BlockSpec((1, tn), lambda j: (0, j))],
        out_specs=pl.BlockSpec((8, tn), lambda j: (0, j)),
        out_shape=jax.ShapeDtypeStruct((8, n), F32),
        compiler_params=_params(("arbitrary",)),
        name="mod",
    )(c8, w_mod, b_mod.reshape(1, n))


def _in_weights_kernel(w_ref, main_ref, ab_ref, *, s_ab, s_br):
    x = w_ref[...]
    main_ref[:, :s_ab] = x[:, :s_ab].astype(BF16)
    main_ref[:, s_ab:] = x[:, s_br:].astype(BF16)
    gate = x[:, s_ab:s_br]
    fill = jnp.zeros((gate.shape[0], LANES - gate.shape[1]), F32)
    ab_ref[...] = jnp.concatenate([gate, fill], axis=1).astype(BF16)


def _in_weights(w_in0, s_ab, s_br, tr=128):
    d, cols = w_in0.shape
    n_main = cols - (s_br - s_ab)
    return pl.pallas_call(
        functools.partial(_in_weights_kernel, s_ab=s_ab, s_br=s_br),
        grid=(d // tr,),
        in_specs=[pl.BlockSpec((tr, cols), lambda i: (i, 0))],
        out_specs=[pl.BlockSpec((tr, n_main), lambda i: (i, 0)),
                   pl.BlockSpec((tr, LANES), lambda i: (i, 0))],
        out_shape=[jax.ShapeDtypeStruct((d, n_main), BF16), jax.ShapeDtypeStruct((d, LANES), BF16)],
        compiler_params=_params(("arbitrary",)),
        name="in_weights",
    )(w_in0)


def _norm_proj_kernel(x_ref, g_ref, sh_ref, sc_ref, w_ref, *rest, with_ab, with_h):
    rest = list(rest)
    wab_ref = rest.pop(0) if with_ab else None
    o_ref = rest.pop(0)
    oab_ref = rest.pop(0) if with_ab else None
    oh_ref = rest.pop(0) if with_h else None
    h_ref = rest.pop(0)

    @pl.when(pl.program_id(1) == 0)
    def _():
        x = x_ref[...]
        ms = jnp.mean(x * x, axis=-1, keepdims=True)
        y = x * lax.rsqrt(ms + EPS) * g_ref[...]
        h = y * (1.0 + sc_ref[0]) + sh_ref[0]
        hb = h.astype(BF16)
        h_ref[...] = hb
        if with_ab:
            oab_ref[...] = _dot(hb, wab_ref[...])
        if with_h:
            oh_ref[...] = h.T.astype(BF16)

    o_ref[...] = _dot(h_ref[...], w_ref[...])


def _norm_proj(x2d, g, mods, shift_chunk, scale_chunk, rows_per_mod, mod_row0, w_bf16, w_ab=None,
               with_h=False, cols=None, tn=512, name="norm_proj"):
    rows, d = x2d.shape
    col0, n = cols if cols is not None else (0, w_bf16.shape[1])
    with_ab = w_ab is not None
    tm = min(1024, rows_per_mod)
    assert rows % tm == 0 and rows_per_mod % tm == 0 and n % tn == 0 and col0 % tn == 0
    jb0 = col0 // tn

    def batch_of_tile(i):
        return mod_row0 + (i * tm) // rows_per_mod

    in_specs = [pl.BlockSpec((tm, d), lambda i, j: (i, 0)),
                pl.BlockSpec((1, d), lambda i, j: (0, 0)),
                pl.BlockSpec((1, 1, d), lambda i, j: (batch_of_tile(i), 0, shift_chunk)),
                pl.BlockSpec((1, 1, d), lambda i, j: (batch_of_tile(i), 0, scale_chunk)),
                pl.BlockSpec((d, tn), lambda i, j: (0, jb0 + j))]
    args = [x2d, g.reshape(1, d), mods, mods, w_bf16]
    out_specs = [pl.BlockSpec((tm, tn), lambda i, j: (i, j))]
    out_shape = [jax.ShapeDtypeStruct((rows, n), F32)]
    if with_ab:
        in_specs.append(pl.BlockSpec((d, LANES), lambda i, j: (0, 0)))
        args.append(w_ab)
        out_specs.append(pl.BlockSpec((tm, LANES), lambda i, j: (i, 0)))
        out_shape.append(jax.ShapeDtypeStruct((rows, LANES), F32))
    if with_h:
        out_specs.append(pl.BlockSpec((d, tm), lambda i, j: (0, i)))
        out_shape.append(jax.ShapeDtypeStruct((d, rows), BF16))
    return pl.pallas_call(
        functools.partial(_norm_proj_kernel, with_ab=with_ab, with_h=with_h),
        grid=(rows // tm, n // tn),
        in_specs=in_specs,
        out_specs=out_specs,
        out_shape=out_shape,
        scratch_shapes=[pltpu.VMEM((tm, d), BF16)],
        compiler_params=_params(("arbitrary", "arbitrary")),
        name=name,
    )(*args)


CONV_ROWS = 4
CONV_CW = 256
CONV_PAD = 16


def _conv_kernel(a_ref, gt_ref, cw_ref, cb_ref, lg_ref, lb_ref, w_ref, o_ref, pad_ref, y_ref,
                 shift_ref):
    c = a_ref.shape[1]
    ncc = c // CONV_CW
    u = a_ref[...] * _sigmoid(gt_ref[...])
    zero = jnp.zeros((CONV_PAD, CONV_CW), F32)
    for r in range(CONV_ROWS):
        for cc in range(ncc):
            pad_ref[r, cc, 0:CONV_PAD, :] = zero
            pad_ref[r, cc, CONV_PAD:CONV_PAD + GRID_W, :] = (
                u[r * GRID_W:(r + 1) * GRID_W, cc * CONV_CW:(cc + 1) * CONV_CW])
            pad_ref[r, cc, CONV_PAD + GRID_W:2 * CONV_PAD + GRID_W, :] = zero

    half = CONV_WIDTH // 2

    sub = 8
    span = GRID_W + 2 * CONV_PAD - sub

    def body(idx, carry):
        r = idx // ncc
        cc = idx % ncc
        for s in range(1, sub):
            shift_ref[s - 1] = pad_ref[r, cc, s:s + span, :]
        acc = jnp.zeros((GRID_W, CONV_CW), F32)
        for k in range(CONV_WIDTH):
            start = CONV_PAD - half + k
            s = start % sub
            base = start - s
            if s == 0:
                src = pad_ref[r, cc, base:base + GRID_W, :]
            else:
                src = shift_ref[s - 1, base:base + GRID_W, :]
            acc = acc + src * cw_ref[cc, k:k + 1, :]
        y_ref[cc, pl.ds(pl.multiple_of(r * GRID_W, GRID_W), GRID_W), :] = acc
        return carry

    lax.fori_loop(0, CONV_ROWS * ncc, body, 0)

    y = jnp.concatenate([y_ref[cc] for cc in range(ncc)], axis=-1) + cb_ref[...]
    mu = jnp.mean(y, axis=-1, keepdims=True)
    yc = y - mu
    var = jnp.mean(yc * yc, axis=-1, keepdims=True)
    yn = yc * lax.rsqrt(var + EPS) * lg_ref[...] + lb_ref[...]
    o_ref[...] = _dot(_silu(yn).astype(BF16), w_ref[...])


def _conformer(p, conv_w, conv_b, ln_g, ln_b, w_out_bf16, c):
    rows = p.shape[0]
    tm = CONV_ROWS * GRID_W
    ncc = c // CONV_CW
    cw = jnp.pad(conv_w, ((0, 32 - CONV_WIDTH), (0, 0)))
    cw = cw.reshape(32, ncc, CONV_CW).transpose(1, 0, 2)
    return pl.pallas_call(
        _conv_kernel,
        grid=(rows // tm,),
        in_specs=[pl.BlockSpec((tm, c), lambda i: (i, 0)),
                  pl.BlockSpec((tm, c), lambda i: (i, 1)),
                  pl.BlockSpec((ncc, 32, CONV_CW), lambda i: (0, 0, 0)),
                  pl.BlockSpec((1, c), lambda i: (0, 0)),
                  pl.BlockSpec((1, c), lambda i: (0, 0)),
                  pl.BlockSpec((1, c), lambda i: (0, 0)),
                  pl.BlockSpec((c, c), lambda i: (0, 0))],
        out_specs=pl.BlockSpec((tm, c), lambda i: (i, 0)),
        out_shape=jax.ShapeDtypeStruct((rows, c), F32),
        scratch_shapes=[pltpu.VMEM((CONV_ROWS, ncc, GRID_W + 2 * CONV_PAD, CONV_CW), F32),
                        pltpu.VMEM((ncc, tm, CONV_CW), F32),
                        pltpu.VMEM((7, GRID_W + 2 * CONV_PAD - 8, CONV_CW), F32)],
        compiler_params=_params(("arbitrary",)),
        name="conformer",
    )(p, p, cw, conv_b.reshape(1, c), ln_g.reshape(1, c), ln_b.reshape(1, c), w_out_bf16)


DN_PAD = 8
DN_RC = 256


def _dnprep_kernel(q_ref, k_ref, v_ref, wq_ref, wk_ref, wv_ref, qo_ref, ko_ref, vo_ref, pad_ref):
    length = q_ref.shape[0]
    zero = jnp.zeros((DN_PAD, LANES), F32)
    half = SHORT_CONV // 2

    def run(x_ref, w_ref, o_ref, mode):
        pad_ref[0:DN_PAD, :] = zero
        pad_ref[DN_PAD + length:2 * DN_PAD + length, :] = zero
        pad_ref[DN_PAD:DN_PAD + length, :] = x_ref[...]
        for r0 in range(0, length, DN_RC):
            acc = jnp.zeros((DN_RC, LANES), F32)
            for k in range(SHORT_CONV):
                start = DN_PAD - half + k + r0
                acc = acc + pad_ref[start:start + DN_RC, :] * w_ref[k:k + 1, :]
            y = _silu(acc)
            if mode != "v":
                y = y * lax.rsqrt(jnp.sum(y * y, axis=-1, keepdims=True) + EPS)
            if mode == "q":
                y = y * (DN_DK ** -0.5)
            o_ref[0, r0:r0 + DN_RC, :] = y.astype(o_ref.dtype)

    run(q_ref, wq_ref, qo_ref, "q")
    run(k_ref, wk_ref, ko_ref, "k")
    run(v_ref, wv_ref, vo_ref, "v")


def _dn_prep(p, col0, short_w8, batch, length):
    nh = DN_HEADS
    out = jax.ShapeDtypeStruct((batch, length, nh * LANES), BF16)
    ospec = pl.BlockSpec((1, length, LANES), lambda b, h: (b, 0, h))
    return pl.pallas_call(
        _dnprep_kernel,
        grid=(batch, nh),
        in_specs=[pl.BlockSpec((length, LANES), lambda b, h: (b, col0 + h)),
                  pl.BlockSpec((length, LANES), lambda b, h: (b, col0 + nh + h)),
                  pl.BlockSpec((length, LANES), lambda b, h: (b, col0 + 2 * nh + h)),
                  pl.BlockSpec((8, LANES), lambda b, h: (0, h)),
                  pl.BlockSpec((8, LANES), lambda b, h: (0, nh + h)),
                  pl.BlockSpec((8, LANES), lambda b, h: (0, 2 * nh + h))],
        out_specs=[ospec, ospec, ospec],
        out_shape=[out, out, out],
        scratch_shapes=[pltpu.VMEM((length + 2 * DN_PAD, LANES), F32)],
        compiler_params=_params(("arbitrary", "arbitrary")),
        name="dn_prep",
    )(p, p, p, short_w8, short_w8, short_w8)


DN_HB = 8
DN_BASE = 8


def _scan_kernel(qc_ref, kc_ref, vc_ref, abc_ref, ql_ref, kl_ref, vl_ref, abl_ref,
                 alog_ref, dtb_ref, o_ref, s_ref):
    ch = DN_CHUNK
    hb = DN_HB
    hg = pl.program_id(1)
    d = pl.program_id(2)
    fwd = d == 0

    row = lax.broadcasted_iota(jnp.int32, (ch, ch), 0)
    col = lax.broadcasted_iota(jnp.int32, (ch, ch), 1)
    diff = jnp.where(fwd, row - col, col - row)
    incl = diff >= 0
    strict = diff > 0
    tri = incl.astype(F32)

    def same_block(size):
        shift = size.bit_length() - 1
        return (row >> shift) == (col >> shift)

    base_mask = same_block(DN_BASE) & strict
    level_masks = []
    size = DN_BASE
    while size < ch:
        level_masks.append(same_block(2 * size) & jnp.logical_not(same_block(size)) & strict)
        size *= 2

    lane0 = 2 * DN_HEADS * d + hg * hb

    neg_a = -jnp.exp(alog_ref[...])
    dtb = dtb_ref[...]

    s_ref[...] = jnp.zeros_like(s_ref)

    heads = range(hb)

    def each(fn, *lists):
        return [fn(*args) for args in zip(*lists)]

    def inverse_minus_identity(ms):
        m0 = each(lambda m: jnp.where(base_mask, m, 0.0), ms)
        m016 = each(lambda t: t.astype(BF16), m0)
        a = each(lambda t: _dot(t, t), m016)
        a16 = each(lambda t: t.astype(BF16), a)
        b = each(lambda t: _dot(t, t), a16)
        f = each(lambda a_, m0_, m16, a16_: a_ - m0_ - _dot(m16, a16_), a, m0, m016, a16)
        e = each(lambda f_, b_: f_ + b_ + _dot(f_.astype(BF16), b_.astype(BF16)), f, b)
        for mask in level_masks:
            c = each(lambda m: jnp.where(mask, m, 0.0), ms)
            tc = each(lambda c_, e_: c_ + _dot(e_.astype(BF16), c_.astype(BF16)), c, e)
            e = each(lambda e_, tc_: e_ - tc_ - _dot(tc_.astype(BF16), e_.astype(BF16)), e, tc)
        return e

    def phase(q_ref, k_ref, v_ref, ab_ref, n_chunks, write_out):
        def body(p, carry):
            ci = jnp.where(fwd, p, n_chunks - 1 - p)
            r0 = pl.multiple_of(ci * ch, ch)
            ab = ab_ref[0, pl.ds(r0, ch), :]
            x = ab + dtb
            softplus = jnp.maximum(x, 0.0) + jnp.log(1.0 + jnp.exp(-jnp.abs(x)))
            g_all = neg_a * softplus
            b_all = _sigmoid(ab)
            gcum_all = _dot(tri, g_all, HI)
            gcum_t = gcum_all.T
            gtot_all = jnp.sum(g_all, axis=0, keepdims=True)
            q16 = [q_ref[0, pl.ds(r0, ch), c * ch:(c + 1) * ch] for c in heads]
            k16 = [k_ref[0, pl.ds(r0, ch), c * ch:(c + 1) * ch] for c in heads]
            q = each(lambda t: t.astype(F32), q16)
            k = each(lambda t: t.astype(F32), k16)
            v = [v_ref[0, pl.ds(r0, ch), c * ch:(c + 1) * ch].astype(F32) for c in heads]
            s = [s_ref[c] for c in heads]
            la = [lane0 + c for c in heads]
            gc_col = [jnp.sum(jnp.where(col == l, gcum_all, 0.0), axis=1, keepdims=True)
                      for l in la]
            beta_col = [jnp.sum(jnp.where(col == l + DN_HEADS, b_all, 0.0), axis=1, keepdims=True)
                        for l in la]
            gc_row = [jnp.sum(jnp.where(row == l, gcum_t, 0.0), axis=0, keepdims=True) for l in la]
            gt = [jnp.sum(jnp.where(col[0:1] == l, gtot_all, 0.0), axis=1, keepdims=True)
                  for l in la]
            decay = each(lambda gc, gr: jnp.where(incl, jnp.exp(gc - gr), 0.0), gc_col, gc_row)
            eg = each(jnp.exp, gc_col)
            if write_out:
                qk = each(lambda q_, k_: _dot_nt(jnp.concatenate([q_, k_], axis=0), k_), q16, k16)
                attn = each(lambda t, dc: t[:ch] * dc, qk, decay)
                kk = each(lambda t: t[ch:], qk)
            else:
                kk = each(lambda k_: _dot_nt(k_, k_), k16)
            m = each(lambda kk_, b_, dc: jnp.where(strict, kk_ * b_ * dc, 0.0), kk, beta_col, decay)
            e = inverse_minus_identity(m)
            rhs = each(lambda v_, k_, b_, eg_: jnp.concatenate([v_ * b_, k_ * (b_ * eg_)], axis=1),
                       v, k, beta_col, eg)
            uw = each(lambda r, e_: r + _dot(e_.astype(BF16), r.astype(BF16)), rhs, e)
            u = each(lambda t: t[:, :LANES], uw)
            w = each(lambda t: t[:, LANES:], uw)
            s16 = each(lambda t: t.astype(BF16), s)
            kdt = each(lambda k_, gt_, gr: k_.T * jnp.exp(gt_ - gr), k, gt, gc_row)
            if write_out:
                ws = each(lambda w_, q_, eg_, s_: _dot(
                    jnp.concatenate([w_, q_ * eg_], axis=0).astype(BF16), s_), w, q, eg, s16)
                v_new = each(lambda u_, ws_: u_ - ws_[:ch], u, ws)
                r2 = each(lambda a_, kd, vn: _dot(
                    jnp.concatenate([a_, kd], axis=0).astype(BF16), vn.astype(BF16)),
                    attn, kdt, v_new)
                o = each(lambda ws_, r2_: ws_[ch:] + r2_[:ch], ws, r2)
                ds = each(lambda t: t[ch:], r2)
            else:
                v_new = each(lambda u_, w_, s_: u_ - _dot(w_.astype(BF16), s_), u, w, s16)
                ds = each(lambda kd, vn: _dot(kd.astype(BF16), vn.astype(BF16)), kdt, v_new)
            s_new = each(lambda s_, gt_, ds_: s_ * jnp.exp(gt_) + ds_, s, gt, ds)
            for c in heads:
                if write_out:
                    o_ref[0, 0, pl.ds(r0, ch), c * ch:(c + 1) * ch] = o[c]
                s_ref[c] = s_new[c]
            return carry

        lax.fori_loop(0, n_chunks, body, 0)

    phase(qc_ref, kc_ref, vc_ref, abc_ref, qc_ref.shape[1] // ch, False)
    phase(ql_ref, kl_ref, vl_ref, abl_ref, ql_ref.shape[1] // ch, True)


def _dn_scan(qc, kc, vc, abc, ql, kl, vl, abl, alog_l, dtb_l):
    batch, lc, hv = qc.shape
    ll = ql.shape[1]
    hb = DN_HB
    w = hb * LANES
    cspec = pl.BlockSpec((1, lc, w), lambda b, g, d: (b, 0, g))
    lspec = pl.BlockSpec((1, ll, w), lambda b, g, d: (b, 0, g))
    return pl.pallas_call(
        _scan_kernel,
        grid=(batch, DN_HEADS // hb, 2),
        in_specs=[cspec, cspec, cspec,
                  pl.BlockSpec((1, lc, LANES), lambda b, g, d: (b, 0, 0)),
                  lspec, lspec, lspec,
                  pl.BlockSpec((1, ll, LANES), lambda b, g, d: (b, 0, 0)),
                  pl.BlockSpec((1, LANES), lambda b, g, d: (0, 0)),
                  pl.BlockSpec((1, LANES), lambda b, g, d: (0, 0))],
        out_specs=pl.BlockSpec((1, 1, ll, w), lambda b, g, d: (d, b, 0, g)),
        out_shape=jax.ShapeDtypeStruct((2, batch, ll, hv), F32),
        scratch_shapes=[pltpu.VMEM((hb, DN_DK, LANES), F32)],
        compiler_params=_params(("arbitrary", "arbitrary", "arbitrary")),
        name="dn_scan",
    )(qc, kc, vc, abc, ql, kl, vl, abl, alog_l, dtb_l)


def _merge_kernel(yc_ref, of_ref, ob_ref, z_ref, brc_ref, brd_ref, g_ref, wdn_ref, wout_ref,
                  x_ref, m2_ref, out_ref):
    o = of_ref[0] + ob_ref[0]
    parts = []
    for h in range(DN_HEADS):
        oh = o[:, h * LANES:(h + 1) * LANES]
        ms = jnp.mean(oh * oh, axis=-1, keepdims=True)
        parts.append(oh * lax.rsqrt(ms + EPS))
    on = jnp.concatenate(parts, axis=-1) * g_ref[...]
    gated = on * _silu(z_ref[...])
    y_dn = _dot(gated.astype(BF16), wdn_ref[...])
    mix = _sigmoid(brc_ref[...]) * yc_ref[...] + _sigmoid(brd_ref[...]) * y_dn
    y = _dot(mix.astype(BF16), wout_ref[...])
    out_ref[...] = x_ref[...] + m2_ref[0] * y


def _merge(y_conv, o2, p, zcol, brcol, norm_g_l, wdn, wout, x2d, mods, seq, tm=256):
    rows, d = x2d.shape
    rspec = lambda cb: pl.BlockSpec((tm, d), lambda i: (i, cb))
    ospec = lambda dr: pl.BlockSpec((1, tm, d), lambda i: (dr, i, 0))
    wspec = pl.BlockSpec((d, d), lambda i: (0, 0), pipeline_mode=pl.Buffered(1))
    return pl.pallas_call(
        _merge_kernel,
        grid=(rows // tm,),
        in_specs=[rspec(0), ospec(0), ospec(1), rspec(zcol), rspec(brcol), rspec(brcol + 1),
                  pl.BlockSpec((1, d), lambda i: (0, 0)), wspec, wspec, rspec(0),
                  pl.BlockSpec((1, 1, d), lambda i: ((i * tm) // seq, 0, 2))],
        out_specs=rspec(0),
        out_shape=jax.ShapeDtypeStruct((rows, d), F32),
        compiler_params=_params(("arbitrary",)),
        name="merge",
    )(y_conv, o2, o2, p, p, p, norm_g_l, wdn, wout, x2d, mods)


ROUTE_TM = 256


def _top_values(vals, count):
    outs = []
    for _ in range(count):
        m = jnp.max(vals, axis=0, keepdims=True)
        outs.append(m)
        vals = jnp.where(vals == m, NEG_BIG, vals)
    return outs


def _route_kernel(q_ref, k1_ref, k2_ref, thr_ref, s2_ref, p1_ref, p2_ref):
    q = q_ref[...]
    half = PEER_KEYS
    n = PEER_TOPK + 1
    for h in range(PEER_HEADS):
        q1 = q[:, (2 * h) * half:(2 * h + 1) * half]
        q2 = q[:, (2 * h + 1) * half:(2 * h + 2) * half]
        s1 = _dot_nt(k1_ref[h], q1, HI)
        s2 = _dot_nt(k2_ref[h], q2, HI)
        v1 = _top_values(s1, n)
        v2 = _top_values(s2, n)
        v2s = jnp.concatenate(v2, axis=0)
        counts = [n // (a + 1) for a in range(n)]
        pad = -sum(counts) % 8
        tokens = v2s.shape[1]
        second = jnp.concatenate([v2s[0:c] for c in counts]
                                 + [jnp.full((pad, tokens), NEG_BIG, F32)], axis=0)
        cand = jnp.concatenate([v1[a] + v2s[0:c] for a, c in enumerate(counts)]
                               + [jnp.full((pad, tokens), NEG_BIG, F32)], axis=0)
        tops = _top_values(cand, n)
        cut = 0.5 * (tops[PEER_TOPK - 1] + tops[PEER_TOPK])
        need = jnp.concatenate([jnp.broadcast_to(cut - v1[a], (c, tokens))
                                for a, c in enumerate(counts)]
                               + [jnp.full((pad, tokens), -NEG_BIG, F32)], axis=0)
        best = v1[0] + v2[0]
        zsum = jnp.sum(jnp.where(second >= need, jnp.exp(cand - best), 0.0), axis=0,
                       keepdims=True)
        thr_ref[h] = cut - s1
        s2_ref[h] = s2
        p1_ref[h] = jnp.exp(s1 - v1[0]) / zsum
        p2_ref[h] = jnp.exp(s2 - v2[0])


def _route(q2d, key1, key2):
    rows, d = q2d.shape
    tm = ROUTE_TM
    big = jax.ShapeDtypeStruct((PEER_HEADS, PEER_KEYS, rows), F32)
    bspec = pl.BlockSpec((PEER_HEADS, PEER_KEYS, tm), lambda i: (0, 0, i))
    kspec = pl.BlockSpec((PEER_HEADS, PEER_KEYS, PEER_KEYS), lambda i: (0, 0, 0))
    return pl.pallas_call(
        _route_kernel,
        grid=(rows // tm,),
        in_specs=[pl.BlockSpec((tm, d), lambda i: (i, 0)), kspec, kspec],
        out_specs=[bspec, bspec, bspec, bspec],
        out_shape=[big, big, big, big],
        compiler_params=_params(("arbitrary",)),
        name="peer_route",
    )(q2d, key1, key2)


PEER_TM = 512
PEER_TE = 512
PEER_WC = 256
GELU_C = 0.7978845608028654


def _experts_kernel(ht_ref, u_ref, vt_ref, thr_ref, s2_ref, p1_ref, p2_ref, o_ref, acc_ref, g_ref):
    e = pl.program_id(1)
    na = PEER_TE // PEER_KEYS

    @pl.when(e == 0)
    def _():
        acc_ref[...] = jnp.zeros_like(acc_ref)

    a_t = _dot(u_ref[...], ht_ref[...])
    row0 = (e % 2) * na
    for al in range(na):
        rows = slice(al * PEER_KEYS, (al + 1) * PEER_KEYS)
        for tc in range(PEER_TM // PEER_WC):
            ts = slice(tc * PEER_WC, (tc + 1) * PEER_WC)
            wd = jnp.zeros((PEER_KEYS, PEER_WC), F32)
            for h in range(PEER_HEADS):
                need = thr_ref[h, pl.ds(row0 + al, 1), ts]
                p1r = p1_ref[h, pl.ds(row0 + al, 1), ts]
                wd = wd + jnp.where(s2_ref[h, :, ts] >= need, p1r * p2_ref[h, :, ts], 0.0)
            x = a_t[rows, ts]
            inner = x * (GELU_C + (GELU_C * 0.044715) * (x * x))
            hx = 0.5 * x
            g_ref[rows, ts] = ((hx + hx * jnp.tanh(inner)) * wd).astype(BF16)
    acc_ref[...] += _dot(vt_ref[...], g_ref[...])

    @pl.when(e == pl.num_programs(1) - 1)
    def _():
        o_ref[...] = acc_ref[...]


def _experts(h_t, u_bf16, vt_bf16, thr, s2, p1, p2):
    d, rows = h_t.shape
    ne = u_bf16.shape[0]
    tm, te = PEER_TM, PEER_TE
    na = te // PEER_KEYS
    assert 2 * na == 8 and (ne // te) % 2 == 0
    bspec = pl.BlockSpec((PEER_HEADS, PEER_KEYS, tm), lambda i, e: (0, 0, i))
    aspec = pl.BlockSpec((PEER_HEADS, 2 * na, tm), lambda i, e: (0, e // 2, i))
    return pl.pallas_call(
        _experts_kernel,
        grid=(rows // tm, ne // te),
        in_specs=[pl.BlockSpec((d, tm), lambda i, e: (0, i)),
                  pl.BlockSpec((te, d), lambda i, e: (e, 0)),
                  pl.BlockSpec((d, te), lambda i, e: (0, e)),
                  aspec, bspec, aspec, bspec],
        out_specs=pl.BlockSpec((d, tm), lambda i, e: (0, i)),
        out_shape=jax.ShapeDtypeStruct((d, rows), F32),
        scratch_shapes=[pltpu.VMEM((d, tm), F32), pltpu.VMEM((te, tm), BF16)],
        compiler_params=_params(("arbitrary", "arbitrary")),
        name="peer_experts",
    )(h_t, u_bf16, vt_bf16, thr, s2, p1, p2)


def _tables_kernel(u_ref, v_ref, u16_ref, vt16_ref):
    u16_ref[...] = u_ref[...].astype(BF16)
    vt16_ref[...] = v_ref[...].T.astype(BF16)


def _expert_tables(peer_u, peer_v):
    ne, d = peer_u.shape
    te = PEER_TE
    return pl.pallas_call(
        _tables_kernel,
        grid=(ne // te,),
        in_specs=[pl.BlockSpec((te, d), lambda e: (e, 0)), pl.BlockSpec((te, d), lambda e: (e, 0))],
        out_specs=[pl.BlockSpec((te, d), lambda e: (e, 0)), pl.BlockSpec((d, te), lambda e: (0, e))],
        out_shape=[jax.ShapeDtypeStruct((ne, d), BF16), jax.ShapeDtypeStruct((d, ne), BF16)],
        compiler_params=_params(("arbitrary",)),
        name="peer_tables",
    )(peer_u, peer_v)


def _final_kernel(x_ref, yt_ref, m5_ref, g_ref, o_ref):
    x = x_ref[...] + m5_ref[0] * yt_ref[...].T
    ms = jnp.mean(x * x, axis=-1, keepdims=True)
    o_ref[...] = x * lax.rsqrt(ms + EPS) * g_ref[...]


def _final(x1, y_t, mods, final_g, seq, tm=512):
    rows, d = x1.shape
    rspec = pl.BlockSpec((tm, d), lambda i: (i, 0))
    return pl.pallas_call(
        _final_kernel,
        grid=(rows // tm,),
        in_specs=[rspec, pl.BlockSpec((d, tm), lambda i: (0, i)),
                  pl.BlockSpec((1, 1, d), lambda i: ((i * tm) // seq, 0, 5)),
                  pl.BlockSpec((1, d), lambda i: (0, 0))],
        out_specs=rspec,
        out_shape=jax.ShapeDtypeStruct((rows, d), F32),
        compiler_params=_params(("arbitrary",)),
        name="final",
    )(x1, y_t, mods, final_g.reshape(1, d))


def kernel(x, c, ctx, c_ctx, w_mod, b_mod, norm1_g, norm2_g, w_in, conv_w, conv_b, conv_ln_g,
           conv_ln_b, w_conv_out, dn_short_w, dn_a_log, dn_dt_bias, dn_norm_g, w_dn_out, w_out,
           peer_w_q, peer_key1, peer_key2, peer_u, peer_v, final_g):
    batch, seq, d = x.shape
    lc = ctx.shape[1]
    assert w_mod.shape[0] == 1, "single-layer block"
    cdim = conv_w.shape[2]
    nqk = DN_HEADS * DN_DK
    col_glu = 2 * cdim
    col_qkv = 3 * nqk
    col_ab = 4 * DN_HEADS
    s_qkv = col_glu
    s_z = s_qkv + col_qkv
    s_ab = s_z + nqk
    s_br = s_ab + col_ab

    c8 = jnp.zeros((8, d), F32).at[:batch].set(c).at[batch].set(c_ctx)
    mods = _mod(c8, w_mod[0], b_mod[0]).reshape(8, 1, N_MOD * d)

    w_main, w_ab = _in_weights(w_in[0], s_ab, s_br)

    x2d = x.reshape(batch * seq, d)
    p, ab_l = _norm_proj(x2d, norm1_g[0], mods, 0, 1, seq, 0, w_main, w_ab=w_ab, name="in_proj")
    pc, ab_c = _norm_proj(ctx.reshape(batch * lc, d), norm1_g[0], mods, 0, 1, batch * lc, batch,
                          w_main, w_ab=w_ab, cols=(s_qkv, col_qkv), name="in_proj_ctx")

    y_conv = _conformer(p, conv_w[0], conv_b[0], conv_ln_g[0], conv_ln_b[0],
                        w_conv_out[0].astype(BF16), cdim)

    short_w8 = jnp.pad(dn_short_w[0], ((0, 8 - SHORT_CONV), (0, 0)))
    ql, kl, vl = _dn_prep(p, s_qkv // LANES, short_w8, batch, seq)
    qc, kc, vc = _dn_prep(pc, 0, short_w8, batch, lc)
    gate_lanes = lambda t: jnp.pad(
        jnp.pad(t, ((0, 0), (0, DN_HEADS))).reshape(1, col_ab), ((0, 0), (0, LANES - col_ab)))
    o_l = _dn_scan(qc, kc, vc, ab_c.reshape(batch, lc, LANES), ql, kl, vl,
                   ab_l.reshape(batch, seq, LANES), gate_lanes(dn_a_log[0]),
                   gate_lanes(dn_dt_bias[0]))

    zcol = s_z // d
    brcol = (s_br - col_ab) // d
    x1 = _merge(y_conv, o_l.reshape(2, batch * seq, nqk), p, zcol, brcol,
                jnp.tile(dn_norm_g[0], DN_HEADS).reshape(1, nqk), w_dn_out[0].astype(BF16),
                w_out[0].astype(BF16), x2d, mods, seq)

    q2d, h2_t = _norm_proj(x1, norm2_g[0], mods, 3, 4, seq, 0, peer_w_q[0].astype(BF16),
                           with_h=True, name="peer_q")
    thr, s2, p1, p2 = _route(q2d, peer_key1[0], peer_key2[0])
    u16, vt16 = _expert_tables(peer_u[0], peer_v[0])
    y_t = _experts(h2_t, u16, vt16, thr, s2, p1, p2)
    out = _final(x1, y_t, mods, final_g, seq)
    return out.reshape(batch, seq, d)
```

```python
import functools

import jax
import jax.numpy as jnp
from jax import lax
from jax.experimental import pallas as pl
from jax.experimental.pallas import tpu as pltpu

F32 = jnp.float32
BF16 = jnp.bfloat16
HI = lax.Precision.HIGHEST

EPS = 1e-6
N_MOD = 6
GRID_W = 64
CONV_WIDTH = 31
DN_HEADS = 16
DN_DK = 128
SHORT_CONV = 5
PEER_HEADS = 8
PEER_KEYS = 128
PEER_TOPK = 16

LANES = 128
DN_CHUNK = 128
NEG_BIG = -3.0e38
VMEM_LIMIT = 56 << 20


def _params(sem, vmem_limit=VMEM_LIMIT):
    return pltpu.CompilerParams(dimension_semantics=sem, vmem_limit_bytes=vmem_limit)


def _sigmoid(x):
    return 1.0 / (1.0 + jnp.exp(-x))


def _silu(x):
    return x * _sigmoid(x)


def _dot(a, b, precision=None):
    return jnp.dot(a, b, preferred_element_type=F32, precision=precision)


def _dot_nt(a, b, precision=None):
    return lax.dot_general(a, b, (((1,), (1,)), ((), ())), preferred_element_type=F32,
                           precision=precision)


def _mod_kernel(c_ref, w_ref, b_ref, o_ref):
    c = c_ref[...]
    o_ref[...] = _dot(_silu(c), w_ref[...], HI) + b_ref[...]


def _mod(c8, w_mod, b_mod):
    d, n = w_mod.shape
    tn = 1024
    return pl.pallas_call(
        _mod_kernel,
        grid=(n // tn,),
        in_specs=[pl.BlockSpec((8, d), lambda j: (0, 0)),
                  pl.BlockSpec((d, tn), lambda j: (0, j)),
                  pl.BlockSpec((1, tn), lambda j: (0, j))],
        out_specs=pl.BlockSpec((8, tn), lambda j: (0, j)),
        out_shape=jax.ShapeDtypeStruct((8, n), F32),
        compiler_params=_params(("arbitrary",)),
        name="mod",
    )(c8, w_mod, b_mod.reshape(1, n))


def _in_weights_kernel(rows_ref, gate_ref, main_ref, ab_ref):
    main_ref[...] = rows_ref[...].T.astype(BF16)

    @pl.when(pl.program_id(0) == 0)
    def _():
        gate = gate_ref[...]
        fill = jnp.zeros((LANES - gate.shape[0], gate.shape[1]), F32)
        ab_ref[...] = jnp.concatenate([gate, fill], axis=0).T.astype(BF16)


def _in_weights(w_in0, s_ab, s_br, tn=512):
    d, cols = w_in0.shape
    w_t = w_in0.T
    n_head, n_tail = s_ab // tn, (cols - s_br) // tn
    assert n_head * tn == s_ab and n_tail * tn == cols - s_br and s_ab % 8 == 0 and s_br % 8 == 0

    def first_row(j):
        return pl.multiple_of(jnp.where(j < n_head, j * tn, s_br + (j - n_head) * tn), 8)

    return pl.pallas_call(
        _in_weights_kernel,
        grid=(n_head + n_tail,),
        in_specs=[pl.BlockSpec((pl.Element(tn), pl.Element(d)), lambda j: (first_row(j), 0)),
                  pl.BlockSpec((pl.Element(s_br - s_ab), pl.Element(d)), lambda j: (s_ab, 0))],
        out_specs=[pl.BlockSpec((d, tn), lambda j: (0, j)),
                   pl.BlockSpec((d, LANES), lambda j: (0, 0))],
        out_shape=[jax.ShapeDtypeStruct((d, (n_head + n_tail) * tn), BF16),
                   jax.ShapeDtypeStruct((d, LANES), BF16)],
        compiler_params=_params(("arbitrary",)),
        name="in_weights",
    )(w_t, w_t)


def _norm_proj_kernel(x_ref, g_ref, sh_ref, sc_ref, w_ref, *rest, with_ab, with_h):
    rest = list(rest)
    wab_ref = rest.pop(0) if with_ab else None
    o_ref = rest.pop(0)
    oab_ref = rest.pop(0) if with_ab else None
    oh_ref = rest.pop(0) if with_h else None
    h_ref = rest.pop(0)

    @pl.when(pl.program_id(1) == 0)
    def _():
        x = x_ref[...]
        ms = jnp.mean(x * x, axis=-1, keepdims=True)
        y = x * lax.rsqrt(ms + EPS) * g_ref[...]
        h = y * (1.0 + sc_ref[0]) + sh_ref[0]
        hb = h.astype(BF16)
        h_ref[...] = hb
        if with_ab:
            oab_ref[...] = _dot(hb, wab_ref[...])
        if with_h:
            oh_ref[...] = h.T.astype(BF16)

    o_ref[...] = _dot(h_ref[...], w_ref[...])


def _norm_proj(x2d, g, mods, shift_chunk, scale_chunk, rows_per_mod, mod_row0, w_bf16, w_ab=None,
               with_h=False, cols=None, tn=512, name="norm_proj"):
    rows, d = x2d.shape
    col0, n = cols if cols is not None else (0, w_bf16.shape[1])
    with_ab = w_ab is not None
    tm = min(1024, rows_per_mod)
    assert rows % tm == 0 and rows_per_mod % tm == 0 and n % tn == 0 and col0 % tn == 0
    jb0 = col0 // tn

    def batch_of_tile(i):
        return mod_row0 + (i * tm) // rows_per_mod

    in_specs = [pl.BlockSpec((tm, d), lambda i, j: (i, 0)),
                pl.BlockSpec((1, d), lambda i, j: (0, 0)),
                pl.BlockSpec((1, 1, d), lambda i, j: (batch_of_tile(i), 0, shift_chunk)),
                pl.BlockSpec((1, 1, d), lambda i, j: (batch_of_tile(i), 0, scale_chunk)),
                pl.BlockSpec((d, tn), lambda i, j: (0, jb0 + j))]
    args = [x2d, g.reshape(1, d), mods, mods, w_bf16]
    out_specs = [pl.BlockSpec((tm, tn), lambda i, j: (i, j))]
    out_shape = [jax.ShapeDtypeStruct((rows, n), F32)]
    if with_ab:
        in_specs.append(pl.BlockSpec((d, LANES), lambda i, j: (0, 0)))
        args.append(w_ab)
        out_specs.append(pl.BlockSpec((tm, LANES), lambda i, j: (i, 0)))
        out_shape.append(jax.ShapeDtypeStruct((rows, LANES), F32))
    if with_h:
        out_specs.append(pl.BlockSpec((d, tm), lambda i, j: (0, i)))
        out_shape.append(jax.ShapeDtypeStruct((d, rows), BF16))
    return pl.pallas_call(
        functools.partial(_norm_proj_kernel, with_ab=with_ab, with_h=with_h),
        grid=(rows // tm, n // tn),
        in_specs=in_specs,
        out_specs=out_specs,
        out_shape=out_shape,
        scratch_shapes=[pltpu.VMEM((tm, d), BF16)],
        compiler_params=_params(("arbitrary", "arbitrary")),
        name=name,
    )(*args)


CONV_ROWS = 4
CONV_CW = 256
CONV_PAD = 16


def _conv_kernel(a_ref, gt_ref, cw_ref, cb_ref, lg_ref, lb_ref, w_ref, o_ref, pad_ref, y_ref,
                 shift_ref):
    c = a_ref.shape[1]
    ncc = c // CONV_CW
    u = a_ref[...] * _sigmoid(gt_ref[...])
    zero = jnp.zeros((CONV_PAD, CONV_CW), F32)
    for r in range(CONV_ROWS):
        for cc in range(ncc):
            pad_ref[r, cc, 0:CONV_PAD, :] = zero
            pad_ref[r, cc, CONV_PAD:CONV_PAD + GRID_W, :] = (
                u[r * GRID_W:(r + 1) * GRID_W, cc * CONV_CW:(cc + 1) * CONV_CW])
            pad_ref[r, cc, CONV_PAD + GRID_W:2 * CONV_PAD + GRID_W, :] = zero

    half = CONV_WIDTH // 2

    sub = 8
    span = GRID_W + 2 * CONV_PAD - sub

    def body(idx, carry):
        r = idx // ncc
        cc = idx % ncc
        for s in range(1, sub):
            shift_ref[s - 1] = pad_ref[r, cc, s:s + span, :]
        acc = jnp.zeros((GRID_W, CONV_CW), F32)
        for k in range(CONV_WIDTH):
            start = CONV_PAD - half + k
            s = start % sub
            base = start - s
            if s == 0:
                src = pad_ref[r, cc, base:base + GRID_W, :]
            else:
                src = shift_ref[s - 1, base:base + GRID_W, :]
            acc = acc + src * cw_ref[cc, k:k + 1, :]
        y_ref[cc, pl.ds(pl.multiple_of(r * GRID_W, GRID_W), GRID_W), :] = acc
        return carry

    lax.fori_loop(0, CONV_ROWS * ncc, body, 0)

    y = jnp.concatenate([y_ref[cc] for cc in range(ncc)], axis=-1) + cb_ref[...]
    mu = jnp.mean(y, axis=-1, keepdims=True)
    yc = y - mu
    var = jnp.mean(yc * yc, axis=-1, keepdims=True)
    yn = yc * lax.rsqrt(var + EPS) * lg_ref[...] + lb_ref[...]
    o_ref[...] = _dot(_silu(yn).astype(BF16), w_ref[...])


def _conformer(p, conv_w, conv_b, ln_g, ln_b, w_out_bf16, c):
    rows = p.shape[0]
    tm = CONV_ROWS * GRID_W
    ncc = c // CONV_CW
    cw = jnp.pad(conv_w, ((0, 32 - CONV_WIDTH), (0, 0)))
    cw = cw.reshape(32, ncc, CONV_CW).transpose(1, 0, 2)
    return pl.pallas_call(
        _conv_kernel,
        grid=(rows // tm,),
        in_specs=[pl.BlockSpec((tm, c), lambda i: (i, 0)),
                  pl.BlockSpec((tm, c), lambda i: (i, 1)),
                  pl.BlockSpec((ncc, 32, CONV_CW), lambda i: (0, 0, 0)),
                  pl.BlockSpec((1, c), lambda i: (0, 0)),
                  pl.BlockSpec((1, c), lambda i: (0, 0)),
                  pl.BlockSpec((1, c), lambda i: (0, 0)),
                  pl.BlockSpec((c, c), lambda i: (0, 0))],
        out_specs=pl.BlockSpec((tm, c), lambda i: (i, 0)),
        out_shape=jax.ShapeDtypeStruct((rows, c), F32),
        scratch_shapes=[pltpu.VMEM((CONV_ROWS, ncc, GRID_W + 2 * CONV_PAD, CONV_CW), F32),
                        pltpu.VMEM((ncc, tm, CONV_CW), F32),
                        pltpu.VMEM((7, GRID_W + 2 * CONV_PAD - 8, CONV_CW), F32)],
        compiler_params=_params(("arbitrary",)),
        name="conformer",
    )(p, p, cw, conv_b.reshape(1, c), ln_g.reshape(1, c), ln_b.reshape(1, c), w_out_bf16)


DN_PAD = 8
DN_RC = 256


def _dnprep_kernel(q_ref, k_ref, v_ref, wq_ref, wk_ref, wv_ref, qo_ref, ko_ref, vo_ref, pad_ref):
    length = q_ref.shape[0]
    zero = jnp.zeros((DN_PAD, LANES), F32)
    half = SHORT_CONV // 2

    def run(x_ref, w_ref, o_ref, mode):
        pad_ref[0:DN_PAD, :] = zero
        pad_ref[DN_PAD + length:2 * DN_PAD + length, :] = zero
        pad_ref[DN_PAD:DN_PAD + length, :] = x_ref[...]
        for r0 in range(0, length, DN_RC):
            acc = jnp.zeros((DN_RC, LANES), F32)
            for k in range(SHORT_CONV):
                start = DN_PAD - half + k + r0
                acc = acc + pad_ref[start:start + DN_RC, :] * w_ref[k:k + 1, :]
            y = _silu(acc)
            if mode != "v":
                y = y * lax.rsqrt(jnp.sum(y * y, axis=-1, keepdims=True) + EPS)
            if mode == "q":
                y = y * (DN_DK ** -0.5)
            o_ref[0, r0:r0 + DN_RC, :] = y.astype(o_ref.dtype)

    run(q_ref, wq_ref, qo_ref, "q")
    run(k_ref, wk_ref, ko_ref, "k")
    run(v_ref, wv_ref, vo_ref, "v")


def _dn_prep(p, col0, short_w8, batch, length):
    nh = DN_HEADS
    out = jax.ShapeDtypeStruct((batch, length, nh * LANES), BF16)
    ospec = pl.BlockSpec((1, length, LANES), lambda b, h: (b, 0, h))
    return pl.pallas_call(
        _dnprep_kernel,
        grid=(batch, nh),
        in_specs=[pl.BlockSpec((length, LANES), lambda b, h: (b, col0 + h)),
                  pl.BlockSpec((length, LANES), lambda b, h: (b, col0 + nh + h)),
                  pl.BlockSpec((length, LANES), lambda b, h: (b, col0 + 2 * nh + h)),
                  pl.BlockSpec((8, LANES), lambda b, h: (0, h)),
                  pl.BlockSpec((8, LANES), lambda b, h: (0, nh + h)),
                  pl.BlockSpec((8, LANES), lambda b, h: (0, 2 * nh + h))],
        out_specs=[ospec, ospec, ospec],
        out_shape=[out, out, out],
        scratch_shapes=[pltpu.VMEM((length + 2 * DN_PAD, LANES), F32)],
        compiler_params=_params(("arbitrary", "arbitrary")),
        name="dn_prep",
    )(p, p, p, short_w8, short_w8, short_w8)


DN_HB = 16
DN_SCAN_VMEM = 58 << 20
DN_BASE = 8


def _scan_kernel(qc_ref, kc_ref, vc_ref, abc_ref, ql_ref, kl_ref, vl_ref, abl_ref,
                 alog_ref, dtb_ref, o_ref, s_ref):
    ch = DN_CHUNK
    hb = DN_HB
    hg = pl.program_id(1)
    d = pl.program_id(2)
    fwd = d == 0

    row = lax.broadcasted_iota(jnp.int32, (ch, ch), 0)
    col = lax.broadcasted_iota(jnp.int32, (ch, ch), 1)
    diff = jnp.where(fwd, row - col, col - row)
    incl = diff >= 0
    strict = diff > 0
    tri = incl.astype(F32)

    def same_block(size):
        shift = size.bit_length() - 1
        return (row >> shift) == (col >> shift)

    base_mask = same_block(DN_BASE) & strict
    level_masks = []
    size = DN_BASE
    while size < ch:
        level_masks.append(same_block(2 * size) & jnp.logical_not(same_block(size)) & strict)
        size *= 2

    lane0 = 2 * DN_HEADS * d + hg * hb

    neg_a = -jnp.exp(alog_ref[...])
    dtb = dtb_ref[...]

    s_ref[...] = jnp.zeros_like(s_ref)

    heads = range(hb)

    def each(fn, *lists):
        return [fn(*args) for args in zip(*lists)]

    def inverse_minus_identity(ms):
        m0 = each(lambda m: jnp.where(base_mask, m, 0.0), ms)
        m016 = each(lambda t: t.astype(BF16), m0)
        a = each(lambda t: _dot(t, t), m016)
        a16 = each(lambda t: t.astype(BF16), a)
        b = each(lambda t: _dot(t, t), a16)
        f = each(lambda a_, m0_, m16, a16_: a_ - m0_ - _dot(m16, a16_), a, m0, m016, a16)
        e = each(lambda f_, b_: f_ + b_ + _dot(f_.astype(BF16), b_.astype(BF16)), f, b)
        for mask in level_masks:
            c = each(lambda m: jnp.where(mask, m, 0.0), ms)
            tc = each(lambda c_, e_: c_ + _dot(e_.astype(BF16), c_.astype(BF16)), c, e)
            e = each(lambda e_, tc_: e_ - tc_ - _dot(tc_.astype(BF16), e_.astype(BF16)), e, tc)
        return e

    def phase(q_ref, k_ref, v_ref, ab_ref, n_chunks, write_out):
        def body(p, carry):
            ci = jnp.where(fwd, p, n_chunks - 1 - p)
            r0 = pl.multiple_of(ci * ch, ch)
            ab = ab_ref[0, pl.ds(r0, ch), :]
            x = ab + dtb
            softplus = jnp.maximum(x, 0.0) + jnp.log(1.0 + jnp.exp(-jnp.abs(x)))
            g_all = neg_a * softplus
            b_all = _sigmoid(ab)
            gcum_all = _dot(tri, g_all, HI)
            gcum_t = gcum_all.T
            gtot_all = jnp.sum(g_all, axis=0, keepdims=True)
            q16 = [q_ref[0, pl.ds(r0, ch), c * ch:(c + 1) * ch] for c in heads]
            k16 = [k_ref[0, pl.ds(r0, ch), c * ch:(c + 1) * ch] for c in heads]
            q = each(lambda t: t.astype(F32), q16)
            k = each(lambda t: t.astype(F32), k16)
            v = [v_ref[0, pl.ds(r0, ch), c * ch:(c + 1) * ch].astype(F32) for c in heads]
            s = [s_ref[c] for c in heads]
            la = [lane0 + c for c in heads]
            gc_col = [jnp.sum(jnp.where(col == l, gcum_all, 0.0), axis=1, keepdims=True)
                      for l in la]
            beta_col = [jnp.sum(jnp.where(col == l + DN_HEADS, b_all, 0.0), axis=1, keepdims=True)
                        for l in la]
            gc_row = [jnp.sum(jnp.where(row == l, gcum_t, 0.0), axis=0, keepdims=True) for l in la]
            gt = [jnp.sum(jnp.where(col[0:1] == l, gtot_all, 0.0), axis=1, keepdims=True)
                  for l in la]
            decay = each(lambda gc, gr: jnp.where(incl, jnp.exp(gc - gr), 0.0), gc_col, gc_row)
            eg = each(jnp.exp, gc_col)
            if write_out:
                qk = each(lambda q_, k_: _dot_nt(jnp.concatenate([q_, k_], axis=0), k_), q16, k16)
                attn = each(lambda t, dc: t[:ch] * dc, qk, decay)
                kk = each(lambda t: t[ch:], qk)
            else:
                kk = each(lambda k_: _dot_nt(k_, k_), k16)
            m = each(lambda kk_, b_, dc: jnp.where(strict, kk_ * b_ * dc, 0.0), kk, beta_col, decay)
            e = inverse_minus_identity(m)
            rhs = each(lambda v_, k_, b_, eg_: jnp.concatenate([v_ * b_, k_ * (b_ * eg_)], axis=1),
                       v, k, beta_col, eg)
            uw = each(lambda r, e_: r + _dot(e_.astype(BF16), r.astype(BF16)), rhs, e)
            u = each(lambda t: t[:, :LANES], uw)
            w = each(lambda t: t[:, LANES:], uw)
            s16 = each(lambda t: t.astype(BF16), s)
            kdt = each(lambda k_, gt_, gr: k_.T * jnp.exp(gt_ - gr), k, gt, gc_row)
            if write_out:
                ws = each(lambda w_, q_, eg_, s_: _dot(
                    jnp.concatenate([w_, q_ * eg_], axis=0).astype(BF16), s_), w, q, eg, s16)
                v_new = each(lambda u_, ws_: u_ - ws_[:ch], u, ws)
                r2 = each(lambda a_, kd, vn: _dot(
                    jnp.concatenate([a_, kd], axis=0).astype(BF16), vn.astype(BF16)),
                    attn, kdt, v_new)
                o = each(lambda ws_, r2_: ws_[ch:] + r2_[:ch], ws, r2)
                ds = each(lambda t: t[ch:], r2)
            else:
                v_new = each(lambda u_, w_, s_: u_ - _dot(w_.astype(BF16), s_), u, w, s16)
                ds = each(lambda kd, vn: _dot(kd.astype(BF16), vn.astype(BF16)), kdt, v_new)
            s_new = each(lambda s_, gt_, ds_: s_ * jnp.exp(gt_) + ds_, s, gt, ds)
            for c in heads:
                if write_out:
                    o_ref[0, 0, pl.ds(r0, ch), c * ch:(c + 1) * ch] = o[c].astype(o_ref.dtype)
                s_ref[c] = s_new[c]
            return carry

        lax.fori_loop(0, n_chunks, body, 0)

    phase(qc_ref, kc_ref, vc_ref, abc_ref, qc_ref.shape[1] // ch, False)
    phase(ql_ref, kl_ref, vl_ref, abl_ref, ql_ref.shape[1] // ch, True)


def _dn_scan(qc, kc, vc, abc, ql, kl, vl, abl, alog_l, dtb_l):
    batch, lc, hv = qc.shape
    ll = ql.shape[1]
    hb = DN_HB
    w = hb * LANES
    once = pl.Buffered(1)
    cspec = pl.BlockSpec((1, lc, w), lambda b, g, d: (b, 0, g), pipeline_mode=once)
    lspec = pl.BlockSpec((1, ll, w), lambda b, g, d: (b, 0, g), pipeline_mode=once)
    return pl.pallas_call(
        _scan_kernel,
        grid=(batch, DN_HEADS // hb, 2),
        in_specs=[cspec, cspec, cspec,
                  pl.BlockSpec((1, lc, LANES), lambda b, g, d: (b, 0, 0)),
                  lspec, lspec, lspec,
                  pl.BlockSpec((1, ll, LANES), lambda b, g, d: (b, 0, 0)),
                  pl.BlockSpec((1, LANES), lambda b, g, d: (0, 0)),
                  pl.BlockSpec((1, LANES), lambda b, g, d: (0, 0))],
        out_specs=pl.BlockSpec((1, 1, ll, w), lambda b, g, d: (d, b, 0, g)),
        out_shape=jax.ShapeDtypeStruct((2, batch, ll, hv), BF16),
        scratch_shapes=[pltpu.VMEM((hb, DN_DK, LANES), F32)],
        compiler_params=_params(("arbitrary", "arbitrary", "arbitrary"), DN_SCAN_VMEM),
        name="dn_scan",
    )(qc, kc, vc, abc, ql, kl, vl, abl, alog_l, dtb_l)


def _merge_kernel(yc_ref, of_ref, ob_ref, z_ref, brc_ref, brd_ref, g_ref, wdn_ref, wout_ref,
                  x_ref, m2_ref, out_ref):
    o = of_ref[0].astype(F32) + ob_ref[0].astype(F32)
    parts = []
    for h in range(DN_HEADS):
        oh = o[:, h * LANES:(h + 1) * LANES]
        ms = jnp.mean(oh * oh, axis=-1, keepdims=True)
        parts.append(oh * lax.rsqrt(ms + EPS))
    on = jnp.concatenate(parts, axis=-1) * g_ref[...]
    gated = on * _silu(z_ref[...])
    y_dn = _dot(gated.astype(BF16), wdn_ref[...])
    mix = _sigmoid(brc_ref[...]) * yc_ref[...] + _sigmoid(brd_ref[...]) * y_dn
    y = _dot(mix.astype(BF16), wout_ref[...])
    out_ref[...] = x_ref[...] + m2_ref[0] * y


def _merge(y_conv, o2, p, zcol, brcol, norm_g_l, wdn, wout, x2d, mods, seq, tm=256):
    rows, d = x2d.shape
    rspec = lambda cb: pl.BlockSpec((tm, d), lambda i: (i, cb))
    ospec = lambda dr: pl.BlockSpec((1, tm, d), lambda i: (dr, i, 0))
    wspec = pl.BlockSpec((d, d), lambda i: (0, 0), pipeline_mode=pl.Buffered(1))
    return pl.pallas_call(
        _merge_kernel,
        grid=(rows // tm,),
        in_specs=[rspec(0), ospec(0), ospec(1), rspec(zcol), rspec(brcol), rspec(brcol + 1),
                  pl.BlockSpec((1, d), lambda i: (0, 0)), wspec, wspec, rspec(0),
                  pl.BlockSpec((1, 1, d), lambda i: ((i * tm) // seq, 0, 2))],
        out_specs=rspec(0),
        out_shape=jax.ShapeDtypeStruct((rows, d), F32),
        compiler_params=_params(("arbitrary",)),
        name="merge",
    )(y_conv, o2, o2, p, p, p, norm_g_l, wdn, wout, x2d, mods)


ROUTE_TM = 256


def _top_values(vals, count):
    outs = []
    for _ in range(count):
        m = jnp.max(vals, axis=0, keepdims=True)
        outs.append(m)
        vals = jnp.where(vals == m, NEG_BIG, vals)
    return outs


def _route_kernel(q_ref, k1_ref, k2_ref, thr_ref, s2_ref, p1_ref, p2_ref):
    q = q_ref[...]
    half = PEER_KEYS
    n = PEER_TOPK + 1
    for h in range(PEER_HEADS):
        q1 = q[:, (2 * h) * half:(2 * h + 1) * half]
        q2 = q[:, (2 * h + 1) * half:(2 * h + 2) * half]
        s1 = _dot_nt(k1_ref[h], q1, HI)
        s2 = _dot_nt(k2_ref[h], q2, HI)
        v1 = _top_values(s1, n)
        v2 = _top_values(s2, n)
        v2s = jnp.concatenate(v2, axis=0)
        counts = [n // (a + 1) for a in range(n)]
        pad = -sum(counts) % 8
        tokens = v2s.shape[1]
        second = jnp.concatenate([v2s[0:c] for c in counts]
                                 + [jnp.full((pad, tokens), NEG_BIG, F32)], axis=0)
        cand = jnp.concatenate([v1[a] + v2s[0:c] for a, c in enumerate(counts)]
                               + [jnp.full((pad, tokens), NEG_BIG, F32)], axis=0)
        tops = _top_values(cand, n)
        cut = 0.5 * (tops[PEER_TOPK - 1] + tops[PEER_TOPK])
        need = jnp.concatenate([jnp.broadcast_to(cut - v1[a], (c, tokens))
                                for a, c in enumerate(counts)]
                               + [jnp.full((pad, tokens), -NEG_BIG, F32)], axis=0)
        best = v1[0] + v2[0]
        zsum = jnp.sum(jnp.where(second >= need, jnp.exp(cand - best), 0.0), axis=0,
                       keepdims=True)
        thr_ref[h] = cut - s1
        s2_ref[h] = s2
        p1_ref[h] = jnp.exp(s1 - v1[0]) / zsum
        p2_ref[h] = jnp.exp(s2 - v2[0])


def _route(q2d, key1, key2):
    rows, d = q2d.shape
    tm = ROUTE_TM
    big = jax.ShapeDtypeStruct((PEER_HEADS, PEER_KEYS, rows), F32)
    bspec = pl.BlockSpec((PEER_HEADS, PEER_KEYS, tm), lambda i: (0, 0, i))
    kspec = pl.BlockSpec((PEER_HEADS, PEER_KEYS, PEER_KEYS), lambda i: (0, 0, 0))
    return pl.pallas_call(
        _route_kernel,
        grid=(rows // tm,),
        in_specs=[pl.BlockSpec((tm, d), lambda i: (i, 0)), kspec, kspec],
        out_specs=[bspec, bspec, bspec, bspec],
        out_shape=[big, big, big, big],
        compiler_params=_params(("arbitrary",)),
        name="peer_route",
    )(q2d, key1, key2)


PEER_TM = 512
PEER_TE = 512
PEER_WC = 256
GELU_C = 0.7978845608028654


def _experts_kernel(ht_ref, u_ref, vt_ref, thr_ref, s2_ref, p1_ref, p2_ref, o_ref, acc_ref, g_ref):
    e = pl.program_id(1)
    na = PEER_TE // PEER_KEYS

    @pl.when(e == 0)
    def _():
        acc_ref[...] = jnp.zeros_like(acc_ref)

    a_t = _dot(u_ref[...], ht_ref[...])
    row0 = (e % 2) * na
    for al in range(na):
        rows = slice(al * PEER_KEYS, (al + 1) * PEER_KEYS)
        for tc in range(PEER_TM // PEER_WC):
            ts = slice(tc * PEER_WC, (tc + 1) * PEER_WC)
            wd = jnp.zeros((PEER_KEYS, PEER_WC), F32)
            for h in range(PEER_HEADS):
                need = thr_ref[h, pl.ds(row0 + al, 1), ts]
                p1r = p1_ref[h, pl.ds(row0 + al, 1), ts]
                wd = wd + jnp.where(s2_ref[h, :, ts] >= need, p1r * p2_ref[h, :, ts], 0.0)
            x = a_t[rows, ts]
            inner = x * (GELU_C + (GELU_C * 0.044715) * (x * x))
            hx = 0.5 * x
            g_ref[rows, ts] = ((hx + hx * jnp.tanh(inner)) * wd).astype(BF16)
    acc_ref[...] += _dot(vt_ref[...], g_ref[...])

    @pl.when(e == pl.num_programs(1) - 1)
    def _():
        o_ref[...] = acc_ref[...]


def _experts(h_t, u_bf16, vt_bf16, thr, s2, p1, p2):
    d, rows = h_t.shape
    ne = u_bf16.shape[0]
    tm, te = PEER_TM, PEER_TE
    na = te // PEER_KEYS
    assert 2 * na == 8 and (ne // te) % 2 == 0
    bspec = pl.BlockSpec((PEER_HEADS, PEER_KEYS, tm), lambda i, e: (0, 0, i))
    aspec = pl.BlockSpec((PEER_HEADS, 2 * na, tm), lambda i, e: (0, e // 2, i))
    return pl.pallas_call(
        _experts_kernel,
        grid=(rows // tm, ne // te),
        in_specs=[pl.BlockSpec((d, tm), lambda i, e: (0, i)),
                  pl.BlockSpec((te, d), lambda i, e: (e, 0)),
                  pl.BlockSpec((d, te), lambda i, e: (0, e)),
                  aspec, bspec, aspec, bspec],
        out_specs=pl.BlockSpec((d, tm), lambda i, e: (0, i)),
        out_shape=jax.ShapeDtypeStruct((d, rows), F32),
        scratch_shapes=[pltpu.VMEM((d, tm), F32), pltpu.VMEM((te, tm), BF16)],
        compiler_params=_params(("arbitrary", "arbitrary")),
        name="peer_experts",
    )(h_t, u_bf16, vt_bf16, thr, s2, p1, p2)


def _tables_kernel(u_ref, v_ref, u16_ref, vt16_ref):
    u16_ref[...] = u_ref[...].astype(BF16)
    vt16_ref[...] = v_ref[...].T.astype(BF16)


def _expert_tables(peer_u, peer_v):
    ne, d = peer_u.shape
    te = PEER_TE
    return pl.pallas_call(
        _tables_kernel,
        grid=(ne // te,),
        in_specs=[pl.BlockSpec((te, d), lambda e: (e, 0)), pl.BlockSpec((te, d), lambda e: (e, 0))],
        out_specs=[pl.BlockSpec((te, d), lambda e: (e, 0)), pl.BlockSpec((d, te), lambda e: (0, e))],
        out_shape=[jax.ShapeDtypeStruct((ne, d), BF16), jax.ShapeDtypeStruct((d, ne), BF16)],
        compiler_params=_params(("arbitrary",)),
        name="peer_tables",
    )(peer_u, peer_v)


def _final_kernel(x_ref, yt_ref, m5_ref, g_ref, o_ref):
    x = x_ref[...] + m5_ref[0] * yt_ref[...].T
    ms = jnp.mean(x * x, axis=-1, keepdims=True)
    o_ref[...] = x * lax.rsqrt(ms + EPS) * g_ref[...]


def _final(x1, y_t, mods, final_g, seq, tm=512):
    rows, d = x1.shape
    rspec = pl.BlockSpec((tm, d), lambda i: (i, 0))
    return pl.pallas_call(
        _final_kernel,
        grid=(rows // tm,),
        in_specs=[rspec, pl.BlockSpec((d, tm), lambda i: (0, i)),
                  pl.BlockSpec((1, 1, d), lambda i: ((i * tm) // seq, 0, 5)),
                  pl.BlockSpec((1, d), lambda i: (0, 0))],
        out_specs=rspec,
        out_shape=jax.ShapeDtypeStruct((rows, d), F32),
        compiler_params=_params(("arbitrary",)),
        name="final",
    )(x1, y_t, mods, final_g.reshape(1, d))


def kernel(x, c, ctx, c_ctx, w_mod, b_mod, norm1_g, norm2_g, w_in, conv_w, conv_b, conv_ln_g,
           conv_ln_b, w_conv_out, dn_short_w, dn_a_log, dn_dt_bias, dn_norm_g, w_dn_out, w_out,
           peer_w_q, peer_key1, peer_key2, peer_u, peer_v, final_g):
    batch, seq, d = x.shape
    lc = ctx.shape[1]
    assert w_mod.shape[0] == 1, "single-layer block"
    cdim = conv_w.shape[2]
    nqk = DN_HEADS * DN_DK
    col_glu = 2 * cdim
    col_qkv = 3 * nqk
    col_ab = 4 * DN_HEADS
    s_qkv = col_glu
    s_z = s_qkv + col_qkv
    s_ab = s_z + nqk
    s_br = s_ab + col_ab

    c8 = jnp.zeros((8, d), F32).at[:batch].set(c).at[batch].set(c_ctx)
    mods = _mod(c8, w_mod[0], b_mod[0]).reshape(8, 1, N_MOD * d)

    w_main, w_ab = _in_weights(w_in[0], s_ab, s_br)

    x2d = x.reshape(batch * seq, d)
    p, ab_l = _norm_proj(x2d, norm1_g[0], mods, 0, 1, seq, 0, w_main, w_ab=w_ab, name="in_proj")
    pc, ab_c = _norm_proj(ctx.reshape(batch * lc, d), norm1_g[0], mods, 0, 1, batch * lc, batch,
                          w_main, w_ab=w_ab, cols=(s_qkv, col_qkv), name="in_proj_ctx")

    y_conv = _conformer(p, conv_w[0], conv_b[0], conv_ln_g[0], conv_ln_b[0],
                        w_conv_out[0].astype(BF16), cdim)

    short_w8 = jnp.pad(dn_short_w[0], ((0, 8 - SHORT_CONV), (0, 0)))
    ql, kl, vl = _dn_prep(p, s_qkv // LANES, short_w8, batch, seq)
    qc, kc, vc = _dn_prep(pc, 0, short_w8, batch, lc)
    gate_lanes = lambda t: jnp.pad(
        jnp.pad(t, ((0, 0), (0, DN_HEADS))).reshape(1, col_ab), ((0, 0), (0, LANES - col_ab)))
    o_l = _dn_scan(qc, kc, vc, ab_c.reshape(batch, lc, LANES), ql, kl, vl,
                   ab_l.reshape(batch, seq, LANES), gate_lanes(dn_a_log[0]),
                   gate_lanes(dn_dt_bias[0]))

    zcol = s_z // d
    brcol = (s_br - col_ab) // d
    x1 = _merge(y_conv, o_l.reshape(2, batch * seq, nqk), p, zcol, brcol,
                jnp.tile(dn_norm_g[0], DN_HEADS).reshape(1, nqk), w_dn_out[0].astype(BF16),
                w_out[0].astype(BF16), x2d, mods, seq)

    q2d, h2_t = _norm_proj(x1, norm2_g[0], mods, 3, 4, seq, 0, peer_w_q[0].astype(BF16),
                           with_h=True, name="peer_q")
    thr, s2, p1, p2 = _route(q2d, peer_key1[0], peer_key2[0])
    u16, vt16 = _expert_tables(peer_u[0], peer_v[0])
    y_t = _experts(h2_t, u16, vt16, thr, s2, p1, p2)
    out = _final(x1, y_t, mods, final_g, seq)
    return out.reshape(batch, seq, d)
```

```python
import functools

import jax
import jax.numpy as jnp
from jax import lax
from jax.experimental import pallas as pl
from jax.experimental.pallas import tpu as pltpu

F32 = jnp.float32
BF16 = jnp.bfloat16
HI = lax.Precision.HIGHEST

EPS = 1e-6
N_MOD = 6
GRID_W = 64
CONV_WIDTH = 31
DN_HEADS = 16
DN_DK = 128
SHORT_CONV = 5
PEER_HEADS = 8
PEER_KEYS = 128
PEER_TOPK = 16

LANES = 128
DN_CHUNK = 128
NEG_BIG = -3.0e38
VMEM_LIMIT = 56 << 20


def _params(sem, vmem_limit=VMEM_LIMIT):
    return pltpu.CompilerParams(dimension_semantics=sem, vmem_limit_bytes=vmem_limit)


def _sigmoid(x):
    return 1.0 / (1.0 + jnp.exp(-x))


def _silu(x):
    return x * _sigmoid(x)


def _dot(a, b, precision=None):
    return jnp.dot(a, b, preferred_element_type=F32, precision=precision)


def _dot_nt(a, b, precision=None):
    return lax.dot_general(a, b, (((1,), (1,)), ((), ())), preferred_element_type=F32,
                           precision=precision)


def _mod_kernel(c_ref, w_ref, b_ref, o_ref):
    c = c_ref[...]
    o_ref[...] = _dot(_silu(c), w_ref[...], HI) + b_ref[...]


def _mod(c8, w_mod, b_mod):
    d, n = w_mod.shape
    tn = 1024
    return pl.pallas_call(
        _mod_kernel,
        grid=(n // tn,),
        in_specs=[pl.BlockSpec((8, d), lambda j: (0, 0)),
                  pl.BlockSpec((d, tn), lambda j: (0, j)),
                  pl.BlockSpec((1, tn), lambda j: (0, j))],
        out_specs=pl.BlockSpec((8, tn), lambda j: (0, j)),
        out_shape=jax.ShapeDtypeStruct((8, n), F32),
        compiler_params=_params(("arbitrary",)),
        name="mod",
    )(c8, w_mod, b_mod.reshape(1, n))


def _in_weights_kernel(rows_ref, gate_ref, main_ref, ab_ref):
    main_ref[...] = rows_ref[...].T.astype(BF16)

    @pl.when(pl.program_id(0) == 0)
    def _():
        gate = gate_ref[...]
        fill = jnp.zeros((LANES - gate.shape[0], gate.shape[1]), F32)
        ab_ref[...] = jnp.concatenate([gate, fill], axis=0).T.astype(BF16)


def _in_weights(w_in0, s_ab, s_br, tn=512):
    d, cols = w_in0.shape
    w_t = w_in0.T
    n_head, n_tail = s_ab // tn, (cols - s_br) // tn
    assert n_head * tn == s_ab and n_tail * tn == cols - s_br and s_ab % 8 == 0 and s_br % 8 == 0

    def first_row(j):
        return pl.multiple_of(jnp.where(j < n_head, j * tn, s_br + (j - n_head) * tn), 8)

    return pl.pallas_call(
        _in_weights_kernel,
        grid=(n_head + n_tail,),
        in_specs=[pl.BlockSpec((pl.Element(tn), pl.Element(d)), lambda j: (first_row(j), 0)),
                  pl.BlockSpec((pl.Element(s_br - s_ab), pl.Element(d)), lambda j: (s_ab, 0))],
        out_specs=[pl.BlockSpec((d, tn), lambda j: (0, j)),
                   pl.BlockSpec((d, LANES), lambda j: (0, 0))],
        out_shape=[jax.ShapeDtypeStruct((d, (n_head + n_tail) * tn), BF16),
                   jax.ShapeDtypeStruct((d, LANES), BF16)],
        compiler_params=_params(("arbitrary",)),
        name="in_weights",
    )(w_t, w_t)


def _norm_proj_kernel(x_ref, g_ref, sh_ref, sc_ref, w_ref, *rest, with_ab, with_h):
    rest = list(rest)
    wab_ref = rest.pop(0) if with_ab else None
    o_ref = rest.pop(0)
    oab_ref = rest.pop(0) if with_ab else None
    oh_ref = rest.pop(0) if with_h else None
    h_ref = rest.pop(0)

    @pl.when(pl.program_id(1) == 0)
    def _():
        x = x_ref[...]
        ms = jnp.mean(x * x, axis=-1, keepdims=True)
        y = x * lax.rsqrt(ms + EPS) * g_ref[...]
        h = y * (1.0 + sc_ref[0]) + sh_ref[0]
        hb = h.astype(BF16)
        h_ref[...] = hb
        if with_ab:
            oab_ref[...] = _dot(hb, wab_ref[...])
        if with_h:
            oh_ref[...] = h.T.astype(BF16)

    o_ref[...] = _dot(h_ref[...], w_ref[...])


def _norm_proj(x2d, g, mods, shift_chunk, scale_chunk, rows_per_mod, mod_row0, w_bf16, w_ab=None,
               with_h=False, cols=None, tn=512, name="norm_proj"):
    rows, d = x2d.shape
    col0, n = cols if cols is not None else (0, w_bf16.shape[1])
    with_ab = w_ab is not None
    tm = min(1024, rows_per_mod)
    assert rows % tm == 0 and rows_per_mod % tm == 0 and n % tn == 0 and col0 % tn == 0
    jb0 = col0 // tn

    def batch_of_tile(i):
        return mod_row0 + (i * tm) // rows_per_mod

    in_specs = [pl.BlockSpec((tm, d), lambda i, j: (i, 0)),
                pl.BlockSpec((1, d), lambda i, j: (0, 0)),
                pl.BlockSpec((1, 1, d), lambda i, j: (batch_of_tile(i), 0, shift_chunk)),
                pl.BlockSpec((1, 1, d), lambda i, j: (batch_of_tile(i), 0, scale_chunk)),
                pl.BlockSpec((d, tn), lambda i, j: (0, jb0 + j))]
    args = [x2d, g.reshape(1, d), mods, mods, w_bf16]
    out_specs = [pl.BlockSpec((tm, tn), lambda i, j: (i, j))]
    out_shape = [jax.ShapeDtypeStruct((rows, n), F32)]
    if with_ab:
        in_specs.append(pl.BlockSpec((d, LANES), lambda i, j: (0, 0)))
        args.append(w_ab)
        out_specs.append(pl.BlockSpec((tm, LANES), lambda i, j: (i, 0)))
        out_shape.append(jax.ShapeDtypeStruct((rows, LANES), F32))
    if with_h:
        out_specs.append(pl.BlockSpec((d, tm), lambda i, j: (0, i)))
        out_shape.append(jax.ShapeDtypeStruct((d, rows), BF16))
    return pl.pallas_call(
        functools.partial(_norm_proj_kernel, with_ab=with_ab, with_h=with_h),
        grid=(rows // tm, n // tn),
        in_specs=in_specs,
        out_specs=out_specs,
        out_shape=out_shape,
        scratch_shapes=[pltpu.VMEM((tm, d), BF16)],
        compiler_params=_params(("arbitrary", "arbitrary")),
        name=name,
    )(*args)


CONV_ROWS = 4
CONV_CW = 256
CONV_PAD = 16


def _conv_kernel(a_ref, gt_ref, cw_ref, cb_ref, lg_ref, lb_ref, w_ref, o_ref, pad_ref, y_ref,
                 shift_ref):
    c = a_ref.shape[1]
    ncc = c // CONV_CW
    u = a_ref[...] * _sigmoid(gt_ref[...])
    zero = jnp.zeros((CONV_PAD, CONV_CW), F32)
    for r in range(CONV_ROWS):
        for cc in range(ncc):
            pad_ref[r, cc, 0:CONV_PAD, :] = zero
            pad_ref[r, cc, CONV_PAD:CONV_PAD + GRID_W, :] = (
                u[r * GRID_W:(r + 1) * GRID_W, cc * CONV_CW:(cc + 1) * CONV_CW])
            pad_ref[r, cc, CONV_PAD + GRID_W:2 * CONV_PAD + GRID_W, :] = zero

    half = CONV_WIDTH // 2

    sub = 8
    span = GRID_W + 2 * CONV_PAD - sub

    def body(idx, carry):
        r = idx // ncc
        cc = idx % ncc
        for s in range(1, sub):
            shift_ref[s - 1] = pad_ref[r, cc, s:s + span, :]
        acc = jnp.zeros((GRID_W, CONV_CW), F32)
        for k in range(CONV_WIDTH):
            start = CONV_PAD - half + k
            s = start % sub
            base = start - s
            if s == 0:
                src = pad_ref[r, cc, base:base + GRID_W, :]
            else:
                src = shift_ref[s - 1, base:base + GRID_W, :]
            acc = acc + src * cw_ref[cc, k:k + 1, :]
        y_ref[cc, pl.ds(pl.multiple_of(r * GRID_W, GRID_W), GRID_W), :] = acc
        return carry

    lax.fori_loop(0, CONV_ROWS * ncc, body, 0)

    y = jnp.concatenate([y_ref[cc] for cc in range(ncc)], axis=-1) + cb_ref[...]
    mu = jnp.mean(y, axis=-1, keepdims=True)
    yc = y - mu
    var = jnp.mean(yc * yc, axis=-1, keepdims=True)
    yn = yc * lax.rsqrt(var + EPS) * lg_ref[...] + lb_ref[...]
    o_ref[...] = _dot(_silu(yn).astype(BF16), w_ref[...])


def _conformer(p, conv_w, conv_b, ln_g, ln_b, w_out_bf16, c):
    rows = p.shape[0]
    tm = CONV_ROWS * GRID_W
    ncc = c // CONV_CW
    cw = jnp.pad(conv_w, ((0, 32 - CONV_WIDTH), (0, 0)))
    cw = cw.reshape(32, ncc, CONV_CW).transpose(1, 0, 2)
    return pl.pallas_call(
        _conv_kernel,
        grid=(rows // tm,),
        in_specs=[pl.BlockSpec((tm, c), lambda i: (i, 0)),
                  pl.BlockSpec((tm, c), lambda i: (i, 1)),
                  pl.BlockSpec((ncc, 32, CONV_CW), lambda i: (0, 0, 0)),
                  pl.BlockSpec((1, c), lambda i: (0, 0)),
                  pl.BlockSpec((1, c), lambda i: (0, 0)),
                  pl.BlockSpec((1, c), lambda i: (0, 0)),
                  pl.BlockSpec((c, c), lambda i: (0, 0))],
        out_specs=pl.BlockSpec((tm, c), lambda i: (i, 0)),
        out_shape=jax.ShapeDtypeStruct((rows, c), F32),
        scratch_shapes=[pltpu.VMEM((CONV_ROWS, ncc, GRID_W + 2 * CONV_PAD, CONV_CW), F32),
                        pltpu.VMEM((ncc, tm, CONV_CW), F32),
                        pltpu.VMEM((7, GRID_W + 2 * CONV_PAD - 8, CONV_CW), F32)],
        compiler_params=_params(("arbitrary",)),
        name="conformer",
    )(p, p, cw, conv_b.reshape(1, c), ln_g.reshape(1, c), ln_b.reshape(1, c), w_out_bf16)


DN_PAD = 8
DN_RC = 256


def _dnprep_kernel(q_ref, k_ref, v_ref, wq_ref, wk_ref, wv_ref, qo_ref, ko_ref, vo_ref, pad_ref):
    length = q_ref.shape[0]
    zero = jnp.zeros((DN_PAD, LANES), F32)
    half = SHORT_CONV // 2

    def run(x_ref, w_ref, o_ref, mode):
        pad_ref[0:DN_PAD, :] = zero
        pad_ref[DN_PAD + length:2 * DN_PAD + length, :] = zero
        pad_ref[DN_PAD:DN_PAD + length, :] = x_ref[...]
        for r0 in range(0, length, DN_RC):
            acc = jnp.zeros((DN_RC, LANES), F32)
            for k in range(SHORT_CONV):
                start = DN_PAD - half + k + r0
                acc = acc + pad_ref[start:start + DN_RC, :] * w_ref[k:k + 1, :]
            y = _silu(acc)
            if mode != "v":
                y = y * lax.rsqrt(jnp.sum(y * y, axis=-1, keepdims=True) + EPS)
            if mode == "q":
                y = y * (DN_DK ** -0.5)
            o_ref[0, r0:r0 + DN_RC, :] = y.astype(o_ref.dtype)

    run(q_ref, wq_ref, qo_ref, "q")
    run(k_ref, wk_ref, ko_ref, "k")
    run(v_ref, wv_ref, vo_ref, "v")


def _dn_prep(p, col0, short_w8, batch, length):
    nh = DN_HEADS
    out = jax.ShapeDtypeStruct((batch, length, nh * LANES), BF16)
    ospec = pl.BlockSpec((1, length, LANES), lambda b, h: (b, 0, h))
    return pl.pallas_call(
        _dnprep_kernel,
        grid=(batch, nh),
        in_specs=[pl.BlockSpec((length, LANES), lambda b, h: (b, col0 + h)),
                  pl.BlockSpec((length, LANES), lambda b, h: (b, col0 + nh + h)),
                  pl.BlockSpec((length, LANES), lambda b, h: (b, col0 + 2 * nh + h)),
                  pl.BlockSpec((8, LANES), lambda b, h: (0, h)),
                  pl.BlockSpec((8, LANES), lambda b, h: (0, nh + h)),
                  pl.BlockSpec((8, LANES), lambda b, h: (0, 2 * nh + h))],
        out_specs=[ospec, ospec, ospec],
        out_shape=[out, out, out],
        scratch_shapes=[pltpu.VMEM((length + 2 * DN_PAD, LANES), F32)],
        compiler_params=_params(("arbitrary", "arbitrary")),
        name="dn_prep",
    )(p, p, p, short_w8, short_w8, short_w8)


DN_HB = 16
DN_SCAN_VMEM = 58 << 20
DN_BASE = 8


def _scan_kernel(qc_ref, kc_ref, vc_ref, abc_ref, ql_ref, kl_ref, vl_ref, abl_ref,
                 alog_ref, dtb_ref, o_ref, s_ref):
    ch = DN_CHUNK
    hb = DN_HB
    hg = pl.program_id(1)
    d = pl.program_id(2)
    fwd = d == 0

    row = lax.broadcasted_iota(jnp.int32, (ch, ch), 0)
    col = lax.broadcasted_iota(jnp.int32, (ch, ch), 1)
    diff = jnp.where(fwd, row - col, col - row)
    incl = diff >= 0
    strict = diff > 0
    tri = incl.astype(F32)

    def same_block(size):
        shift = size.bit_length() - 1
        return (row >> shift) == (col >> shift)

    base_mask = same_block(DN_BASE) & strict
    level_masks = []
    size = DN_BASE
    while size < ch:
        level_masks.append(same_block(2 * size) & jnp.logical_not(same_block(size)) & strict)
        size *= 2

    lane0 = 2 * DN_HEADS * d + hg * hb

    neg_a = -jnp.exp(alog_ref[...])
    dtb = dtb_ref[...]

    s_ref[...] = jnp.zeros_like(s_ref)

    heads = range(hb)

    def each(fn, *lists):
        return [fn(*args) for args in zip(*lists)]

    def inverse_minus_identity(ms):
        m0 = each(lambda m: jnp.where(base_mask, m, 0.0), ms)
        m016 = each(lambda t: t.astype(BF16), m0)
        a = each(lambda t: _dot(t, t), m016)
        a16 = each(lambda t: t.astype(BF16), a)
        b = each(lambda t: _dot(t, t), a16)
        f = each(lambda a_, m0_, m16, a16_: a_ - m0_ - _dot(m16, a16_), a, m0, m016, a16)
        e = each(lambda f_, b_: f_ + b_ + _dot(f_.astype(BF16), b_.astype(BF16)), f, b)
        for mask in level_masks:
            c = each(lambda m: jnp.where(mask, m, 0.0), ms)
            tc = each(lambda c_, e_: c_ + _dot(e_.astype(BF16), c_.astype(BF16)), c, e)
            e = each(lambda e_, tc_: e_ - tc_ - _dot(tc_.astype(BF16), e_.astype(BF16)), e, tc)
        return e

    def phase(q_ref, k_ref, v_ref, ab_ref, n_chunks, write_out):
        def body(p, carry):
            ci = jnp.where(fwd, p, n_chunks - 1 - p)
            r0 = pl.multiple_of(ci * ch, ch)
            ab = ab_ref[0, pl.ds(r0, ch), :]
            x = ab + dtb
            softplus = jnp.maximum(x, 0.0) + jnp.log(1.0 + jnp.exp(-jnp.abs(x)))
            g_all = neg_a * softplus
            b_all = _sigmoid(ab)
            gcum_all = _dot(tri, g_all, HI)
            gcum_t = gcum_all.T
            gtot_all = jnp.sum(g_all, axis=0, keepdims=True)
            q16 = [q_ref[0, pl.ds(r0, ch), c * ch:(c + 1) * ch] for c in heads]
            k16 = [k_ref[0, pl.ds(r0, ch), c * ch:(c + 1) * ch] for c in heads]
            q = each(lambda t: t.astype(F32), q16)
            k = each(lambda t: t.astype(F32), k16)
            v = [v_ref[0, pl.ds(r0, ch), c * ch:(c + 1) * ch].astype(F32) for c in heads]
            s = [s_ref[c] for c in heads]
            la = [lane0 + c for c in heads]
            gc_col = [jnp.sum(jnp.where(col == l, gcum_all, 0.0), axis=1, keepdims=True)
                      for l in la]
            beta_col = [jnp.sum(jnp.where(col == l + DN_HEADS, b_all, 0.0), axis=1, keepdims=True)
                        for l in la]
            gc_row = [jnp.sum(jnp.where(row == l, gcum_t, 0.0), axis=0, keepdims=True) for l in la]
            gt = [jnp.sum(jnp.where(col[0:1] == l, gtot_all, 0.0), axis=1, keepdims=True)
                  for l in la]
            decay = each(lambda gc, gr: jnp.where(incl, jnp.exp(gc - gr), 0.0), gc_col, gc_row)
            eg = each(jnp.exp, gc_col)
            if write_out:
                qk = each(lambda q_, k_: _dot_nt(jnp.concatenate([q_, k_], axis=0), k_), q16, k16)
                attn = each(lambda t, dc: t[:ch] * dc, qk, decay)
                kk = each(lambda t: t[ch:], qk)
            else:
                kk = each(lambda k_: _dot_nt(k_, k_), k16)
            m = each(lambda kk_, b_, dc: jnp.where(strict, kk_ * b_ * dc, 0.0), kk, beta_col, decay)
            e = inverse_minus_identity(m)
            rhs = each(lambda v_, k_, b_, eg_: jnp.concatenate([v_ * b_, k_ * (b_ * eg_)], axis=1),
                       v, k, beta_col, eg)
            uw = each(lambda r, e_: r + _dot(e_.astype(BF16), r.astype(BF16)), rhs, e)
            u = each(lambda t: t[:, :LANES], uw)
            w = each(lambda t: t[:, LANES:], uw)
            s16 = each(lambda t: t.astype(BF16), s)
            kdt = each(lambda k_, gt_, gr: k_.T * jnp.exp(gt_ - gr), k, gt, gc_row)
            if write_out:
                ws = each(lambda w_, q_, eg_, s_: _dot(
                    jnp.concatenate([w_, q_ * eg_], axis=0).astype(BF16), s_), w, q, eg, s16)
                v_new = each(lambda u_, ws_: u_ - ws_[:ch], u, ws)
                r2 = each(lambda a_, kd, vn: _dot(
                    jnp.concatenate([a_, kd], axis=0).astype(BF16), vn.astype(BF16)),
                    attn, kdt, v_new)
                o = each(lambda ws_, r2_: ws_[ch:] + r2_[:ch], ws, r2)
                ds = each(lambda t: t[ch:], r2)
            else:
                v_new = each(lambda u_, w_, s_: u_ - _dot(w_.astype(BF16), s_), u, w, s16)
                ds = each(lambda kd, vn: _dot(kd.astype(BF16), vn.astype(BF16)), kdt, v_new)
            s_new = each(lambda s_, gt_, ds_: s_ * jnp.exp(gt_) + ds_, s, gt, ds)
            for c in heads:
                if write_out:
                    o_ref[0, 0, pl.ds(r0, ch), c * ch:(c + 1) * ch] = o[c].astype(o_ref.dtype)
                s_ref[c] = s_new[c]
            return carry

        lax.fori_loop(0, n_chunks, body, 0)

    phase(qc_ref, kc_ref, vc_ref, abc_ref, qc_ref.shape[1] // ch, False)
    phase(ql_ref, kl_ref, vl_ref, abl_ref, ql_ref.shape[1] // ch, True)


def _dn_scan(qc, kc, vc, abc, ql, kl, vl, abl, alog_l, dtb_l):
    batch, lc, hv = qc.shape
    ll = ql.shape[1]
    hb = DN_HB
    w = hb * LANES
    once = pl.Buffered(1)
    cspec = pl.BlockSpec((1, lc, w), lambda b, g, d: (b, 0, g), pipeline_mode=once)
    lspec = pl.BlockSpec((1, ll, w), lambda b, g, d: (b, 0, g), pipeline_mode=once)
    return pl.pallas_call(
        _scan_kernel,
        grid=(batch, DN_HEADS // hb, 2),
        in_specs=[cspec, cspec, cspec,
                  pl.BlockSpec((1, lc, LANES), lambda b, g, d: (b, 0, 0)),
                  lspec, lspec, lspec,
                  pl.BlockSpec((1, ll, LANES), lambda b, g, d: (b, 0, 0)),
                  pl.BlockSpec((1, LANES), lambda b, g, d: (0, 0)),
                  pl.BlockSpec((1, LANES), lambda b, g, d: (0, 0))],
        out_specs=pl.BlockSpec((1, 1, ll, w), lambda b, g, d: (d, b, 0, g)),
        out_shape=jax.ShapeDtypeStruct((2, batch, ll, hv), BF16),
        scratch_shapes=[pltpu.VMEM((hb, DN_DK, LANES), F32)],
        compiler_params=_params(("arbitrary", "arbitrary", "arbitrary"), DN_SCAN_VMEM),
        name="dn_scan",
    )(qc, kc, vc, abc, ql, kl, vl, abl, alog_l, dtb_l)


def _merge_kernel(yc_ref, of_ref, ob_ref, z_ref, brc_ref, brd_ref, g_ref, wdn_ref, wout_ref,
                  x_ref, m2_ref, out_ref):
    o = of_ref[0].astype(F32) + ob_ref[0].astype(F32)
    parts = []
    for h in range(DN_HEADS):
        oh = o[:, h * LANES:(h + 1) * LANES]
        ms = jnp.mean(oh * oh, axis=-1, keepdims=True)
        parts.append(oh * lax.rsqrt(ms + EPS))
    on = jnp.concatenate(parts, axis=-1) * g_ref[...]
    gated = on * _silu(z_ref[...])
    y_dn = _dot(gated.astype(BF16), wdn_ref[...])
    mix = _sigmoid(brc_ref[...]) * yc_ref[...] + _sigmoid(brd_ref[...]) * y_dn
    y = _dot(mix.astype(BF16), wout_ref[...])
    out_ref[...] = x_ref[...] + m2_ref[0] * y


def _merge(y_conv, o2, p, zcol, brcol, norm_g_l, wdn, wout, x2d, mods, seq, tm=256):
    rows, d = x2d.shape
    rspec = lambda cb: pl.BlockSpec((tm, d), lambda i: (i, cb))
    ospec = lambda dr: pl.BlockSpec((1, tm, d), lambda i: (dr, i, 0))
    wspec = pl.BlockSpec((d, d), lambda i: (0, 0), pipeline_mode=pl.Buffered(1))
    return pl.pallas_call(
        _merge_kernel,
        grid=(rows // tm,),
        in_specs=[rspec(0), ospec(0), ospec(1), rspec(zcol), rspec(brcol), rspec(brcol + 1),
                  pl.BlockSpec((1, d), lambda i: (0, 0)), wspec, wspec, rspec(0),
                  pl.BlockSpec((1, 1, d), lambda i: ((i * tm) // seq, 0, 2))],
        out_specs=rspec(0),
        out_shape=jax.ShapeDtypeStruct((rows, d), F32),
        compiler_params=_params(("arbitrary",)),
        name="merge",
    )(y_conv, o2, o2, p, p, p, norm_g_l, wdn, wout, x2d, mods)


ROUTE_TM = 256


def _top_values(vals, count):
    outs = []
    for _ in range(count):
        m = jnp.max(vals, axis=0, keepdims=True)
        outs.append(m)
        vals = jnp.where(vals == m, NEG_BIG, vals)
    return outs


def _route_kernel(q_ref, k1_ref, k2_ref, thr_ref, s2_ref, p1_ref, p2_ref):
    q = q_ref[...]
    half = PEER_KEYS
    n = PEER_TOPK + 1
    for h in range(PEER_HEADS):
        q1 = q[:, (2 * h) * half:(2 * h + 1) * half]
        q2 = q[:, (2 * h + 1) * half:(2 * h + 2) * half]
        s1 = _dot_nt(k1_ref[h], q1, HI)
        s2 = _dot_nt(k2_ref[h], q2, HI)
        v1 = _top_values(s1, n)
        v2 = _top_values(s2, n)
        v2s = jnp.concatenate(v2, axis=0)
        counts = [n // (a + 1) for a in range(n)]
        pad = -sum(counts) % 8
        tokens = v2s.shape[1]
        second = jnp.concatenate([v2s[0:c] for c in counts]
                                 + [jnp.full((pad, tokens), NEG_BIG, F32)], axis=0)
        cand = jnp.concatenate([v1[a] + v2s[0:c] for a, c in enumerate(counts)]
                               + [jnp.full((pad, tokens), NEG_BIG, F32)], axis=0)
        tops = _top_values(cand, n)
        cut = 0.5 * (tops[PEER_TOPK - 1] + tops[PEER_TOPK])
        need = jnp.concatenate([jnp.broadcast_to(cut - v1[a], (c, tokens))
                                for a, c in enumerate(counts)]
                               + [jnp.full((pad, tokens), -NEG_BIG, F32)], axis=0)
        best = v1[0] + v2[0]
        zsum = jnp.sum(jnp.where(second >= need, jnp.exp(cand - best), 0.0), axis=0,
                       keepdims=True)
        thr_ref[h] = cut - s1
        s2_ref[h] = s2
        p1_ref[h] = jnp.exp(s1 - v1[0]) / zsum
        p2_ref[h] = jnp.exp(s2 - v2[0])


def _route(q2d, key1, key2):
    rows, d = q2d.shape
    tm = ROUTE_TM
    big = jax.ShapeDtypeStruct((PEER_HEADS, PEER_KEYS, rows), F32)
    bspec = pl.BlockSpec((PEER_HEADS, PEER_KEYS, tm), lambda i: (0, 0, i))
    kspec = pl.BlockSpec((PEER_HEADS, PEER_KEYS, PEER_KEYS), lambda i: (0, 0, 0))
    return pl.pallas_call(
        _route_kernel,
        grid=(rows // tm,),
        in_specs=[pl.BlockSpec((tm, d), lambda i: (i, 0)), kspec, kspec],
        out_specs=[bspec, bspec, bspec, bspec],
        out_shape=[big, big, big, big],
        compiler_params=_params(("arbitrary",)),
        name="peer_route",
    )(q2d, key1, key2)


PEER_TM = 512
PEER_TE = 1024
PEER_WC = 256
GELU_C = 0.7978845608028654


def _experts_kernel(ht_ref, u_ref, vt_ref, thr_ref, s2_ref, p1_ref, p2_ref, o_ref, acc_ref, g_ref):
    e = pl.program_id(1)
    na = PEER_TE // PEER_KEYS

    @pl.when(e == 0)
    def _():
        acc_ref[...] = jnp.zeros_like(acc_ref)

    a_t = _dot(u_ref[...], ht_ref[...])
    for al in range(na):
        rows = slice(al * PEER_KEYS, (al + 1) * PEER_KEYS)
        for tc in range(PEER_TM // PEER_WC):
            ts = slice(tc * PEER_WC, (tc + 1) * PEER_WC)
            wd = jnp.zeros((PEER_KEYS, PEER_WC), F32)
            for h in range(PEER_HEADS):
                need = thr_ref[h, al:al + 1, ts]
                p1r = p1_ref[h, al:al + 1, ts]
                wd = wd + jnp.where(s2_ref[h, :, ts] >= need, p1r * p2_ref[h, :, ts], 0.0)
            x = a_t[rows, ts]
            inner = x * (GELU_C + (GELU_C * 0.044715) * (x * x))
            hx = 0.5 * x
            g_ref[rows, ts] = ((hx + hx * jnp.tanh(inner)) * wd).astype(BF16)
    acc_ref[...] += _dot(vt_ref[...], g_ref[...])

    @pl.when(e == pl.num_programs(1) - 1)
    def _():
        o_ref[...] = acc_ref[...]


def _experts(h_t, u_bf16, vt_bf16, thr, s2, p1, p2):
    d, rows = h_t.shape
    ne = u_bf16.shape[0]
    tm, te = PEER_TM, PEER_TE
    na = te // PEER_KEYS
    assert na == 8 and ne % te == 0
    bspec = pl.BlockSpec((PEER_HEADS, PEER_KEYS, tm), lambda i, e: (0, 0, i))
    aspec = pl.BlockSpec((PEER_HEADS, na, tm), lambda i, e: (0, e, i))
    return pl.pallas_call(
        _experts_kernel,
        grid=(rows // tm, ne // te),
        in_specs=[pl.BlockSpec((d, tm), lambda i, e: (0, i)),
                  pl.BlockSpec((te, d), lambda i, e: (e, 0)),
                  pl.BlockSpec((d, te), lambda i, e: (0, e)),
                  aspec, bspec, aspec, bspec],
        out_specs=pl.BlockSpec((d, tm), lambda i, e: (0, i)),
        out_shape=jax.ShapeDtypeStruct((d, rows), F32),
        scratch_shapes=[pltpu.VMEM((d, tm), F32), pltpu.VMEM((te, tm), BF16)],
        compiler_params=_params(("arbitrary", "arbitrary")),
        name="peer_experts",
    )(h_t, u_bf16, vt_bf16, thr, s2, p1, p2)


def _tables_kernel(u_ref, v_ref, u16_ref, vt16_ref):
    u16_ref[...] = u_ref[...].astype(BF16)
    vt16_ref[...] = v_ref[...].T.astype(BF16)


def _expert_tables(peer_u, peer_v):
    ne, d = peer_u.shape
    te = 512
    return pl.pallas_call(
        _tables_kernel,
        grid=(ne // te,),
        in_specs=[pl.BlockSpec((te, d), lambda e: (e, 0)), pl.BlockSpec((te, d), lambda e: (e, 0))],
        out_specs=[pl.BlockSpec((te, d), lambda e: (e, 0)), pl.BlockSpec((d, te), lambda e: (0, e))],
        out_shape=[jax.ShapeDtypeStruct((ne, d), BF16), jax.ShapeDtypeStruct((d, ne), BF16)],
        compiler_params=_params(("arbitrary",)),
        name="peer_tables",
    )(peer_u, peer_v)


def _final_kernel(x_ref, yt_ref, m5_ref, g_ref, o_ref):
    x = x_ref[...] + m5_ref[0] * yt_ref[...].T
    ms = jnp.mean(x * x, axis=-1, keepdims=True)
    o_ref[...] = x * lax.rsqrt(ms + EPS) * g_ref[...]


def _final(x1, y_t, mods, final_g, seq, tm=512):
    rows, d = x1.shape
    rspec = pl.BlockSpec((tm, d), lambda i: (i, 0))
    return pl.pallas_call(
        _final_kernel,
        grid=(rows // tm,),
        in_specs=[rspec, pl.BlockSpec((d, tm), lambda i: (0, i)),
                  pl.BlockSpec((1, 1, d), lambda i: ((i * tm) // seq, 0, 5)),
                  pl.BlockSpec((1, d), lambda i: (0, 0))],
        out_specs=rspec,
        out_shape=jax.ShapeDtypeStruct((rows, d), F32),
        compiler_params=_params(("arbitrary",)),
        name="final",
    )(x1, y_t, mods, final_g.reshape(1, d))


def kernel(x, c, ctx, c_ctx, w_mod, b_mod, norm1_g, norm2_g, w_in, conv_w, conv_b, conv_ln_g,
           conv_ln_b, w_conv_out, dn_short_w, dn_a_log, dn_dt_bias, dn_norm_g, w_dn_out, w_out,
           peer_w_q, peer_key1, peer_key2, peer_u, peer_v, final_g):
    batch, seq, d = x.shape
    lc = ctx.shape[1]
    assert w_mod.shape[0] == 1, "single-layer block"
    cdim = conv_w.shape[2]
    nqk = DN_HEADS * DN_DK
    col_glu = 2 * cdim
    col_qkv = 3 * nqk
    col_ab = 4 * DN_HEADS
    s_qkv = col_glu
    s_z = s_qkv + col_qkv
    s_ab = s_z + nqk
    s_br = s_ab + col_ab

    c8 = jnp.zeros((8, d), F32).at[:batch].set(c).at[batch].set(c_ctx)
    mods = _mod(c8, w_mod[0], b_mod[0]).reshape(8, 1, N_MOD * d)

    w_main, w_ab = _in_weights(w_in[0], s_ab, s_br)

    x2d = x.reshape(batch * seq, d)
    p, ab_l = _norm_proj(x2d, norm1_g[0], mods, 0, 1, seq, 0, w_main, w_ab=w_ab, name="in_proj")
    pc, ab_c = _norm_proj(ctx.reshape(batch * lc, d), norm1_g[0], mods, 0, 1, batch * lc, batch,
                          w_main, w_ab=w_ab, cols=(s_qkv, col_qkv), name="in_proj_ctx")

    y_conv = _conformer(p, conv_w[0], conv_b[0], conv_ln_g[0], conv_ln_b[0],
                        w_conv_out[0].astype(BF16), cdim)

    short_w8 = jnp.pad(dn_short_w[0], ((0, 8 - SHORT_CONV), (0, 0)))
    ql, kl, vl = _dn_prep(p, s_qkv // LANES, short_w8, batch, seq)
    qc, kc, vc = _dn_prep(pc, 0, short_w8, batch, lc)
    gate_lanes = lambda t: jnp.pad(
        jnp.pad(t, ((0, 0), (0, DN_HEADS))).reshape(1, col_ab), ((0, 0), (0, LANES - col_ab)))
    o_l = _dn_scan(qc, kc, vc, ab_c.reshape(batch, lc, LANES), ql, kl, vl,
                   ab_l.reshape(batch, seq, LANES), gate_lanes(dn_a_log[0]),
                   gate_lanes(dn_dt_bias[0]))

    zcol = s_z // d
    brcol = (s_br - col_ab) // d
    x1 = _merge(y_conv, o_l.reshape(2, batch * seq, nqk), p, zcol, brcol,
                jnp.tile(dn_norm_g[0], DN_HEADS).reshape(1, nqk), w_dn_out[0].astype(BF16),
                w_out[0].astype(BF16), x2d, mods, seq)

    q2d, h2_t = _norm_proj(x1, norm2_g[0], mods, 3, 4, seq, 0, peer_w_q[0].astype(BF16),
                           with_h=True, name="peer_q")
    thr, s2, p1, p2 = _route(q2d, peer_key1[0], peer_key2[0])
    u16, vt16 = _expert_tables(peer_u[0], peer_v[0])
    y_t = _experts(h2_t, u16, vt16, thr, s2, p1, p2)
    out = _final(x1, y_t, mods, final_g, seq)
    return out.reshape(batch, seq, d)
```

```python
import functools

import jax
import jax.numpy as jnp
from jax import lax
from jax.experimental import pallas as pl
from jax.experimental.pallas import tpu as pltpu

F32 = jnp.float32
BF16 = jnp.bfloat16
HI = lax.Precision.HIGHEST

EPS = 1e-6
N_MOD = 6
GRID_W = 64
CONV_WIDTH = 31
DN_HEADS = 16
DN_DK = 128
SHORT_CONV = 5
PEER_HEADS = 8
PEER_KEYS = 128
PEER_TOPK = 16

LANES = 128
DN_CHUNK = 128
NEG_BIG = -3.0e38
VMEM_LIMIT = 56 << 20


def _params(sem, vmem_limit=VMEM_LIMIT):
    return pltpu.CompilerParams(dimension_semantics=sem, vmem_limit_bytes=vmem_limit)


def _sigmoid(x):
    return 1.0 / (1.0 + jnp.exp(-x))


def _silu(x):
    return x * _sigmoid(x)


def _dot(a, b, precision=None):
    return jnp.dot(a, b, preferred_element_type=F32, precision=precision)


def _dot_nt(a, b, precision=None):
    return lax.dot_general(a, b, (((1,), (1,)), ((), ())), preferred_element_type=F32,
                           precision=precision)


def _mod_kernel(c_ref, w_ref, b_ref, o_ref):
    c = c_ref[...]
    o_ref[...] = _dot(_silu(c), w_ref[...], HI) + b_ref[...]


def _mod(c8, w_mod, b_mod):
    d, n = w_mod.shape
    tn = 1024
    return pl.pallas_call(
        _mod_kernel,
        grid=(n // tn,),
        in_specs=[pl.BlockSpec((8, d), lambda j: (0, 0)),
                  pl.BlockSpec((d, tn), lambda j: (0, j)),
                  pl.BlockSpec((1, tn), lambda j: (0, j))],
        out_specs=pl.BlockSpec((8, tn), lambda j: (0, j)),
        out_shape=jax.ShapeDtypeStruct((8, n), F32),
        compiler_params=_params(("arbitrary",)),
        name="mod",
    )(c8, w_mod, b_mod.reshape(1, n))


def _in_weights_kernel(rows_ref, gate_ref, main_ref, ab_ref):
    main_ref[...] = rows_ref[...].T.astype(BF16)

    @pl.when(pl.program_id(0) == 0)
    def _():
        gate = gate_ref[...]
        fill = jnp.zeros((LANES - gate.shape[0], gate.shape[1]), F32)
        ab_ref[...] = jnp.concatenate([gate, fill], axis=0).T.astype(BF16)


def _in_weights(w_in0, s_ab, s_br, tn=512):
    d, cols = w_in0.shape
    w_t = w_in0.T
    n_head, n_tail = s_ab // tn, (cols - s_br) // tn
    assert n_head * tn == s_ab and n_tail * tn == cols - s_br and s_ab % 8 == 0 and s_br % 8 == 0

    def first_row(j):
        return pl.multiple_of(jnp.where(j < n_head, j * tn, s_br + (j - n_head) * tn), 8)

    return pl.pallas_call(
        _in_weights_kernel,
        grid=(n_head + n_tail,),
        in_specs=[pl.BlockSpec((pl.Element(tn), pl.Element(d)), lambda j: (first_row(j), 0)),
                  pl.BlockSpec((pl.Element(s_br - s_ab), pl.Element(d)), lambda j: (s_ab, 0))],
        out_specs=[pl.BlockSpec((d, tn), lambda j: (0, j)),
                   pl.BlockSpec((d, LANES), lambda j: (0, 0))],
        out_shape=[jax.ShapeDtypeStruct((d, (n_head + n_tail) * tn), BF16),
                   jax.ShapeDtypeStruct((d, LANES), BF16)],
        compiler_params=_params(("arbitrary",)),
        name="in_weights",
    )(w_t, w_t)


def _norm_proj_kernel(x_ref, g_ref, sh_ref, sc_ref, w_ref, *rest, with_ab, with_h):
    rest = list(rest)
    wab_ref = rest.pop(0) if with_ab else None
    o_ref = rest.pop(0)
    oab_ref = rest.pop(0) if with_ab else None
    oh_ref = rest.pop(0) if with_h else None
    h_ref = rest.pop(0)

    @pl.when(pl.program_id(1) == 0)
    def _():
        x = x_ref[...]
        ms = jnp.mean(x * x, axis=-1, keepdims=True)
        y = x * lax.rsqrt(ms + EPS) * g_ref[...]
        h = y * (1.0 + sc_ref[0]) + sh_ref[0]
        hb = h.astype(BF16)
        h_ref[...] = hb
        if with_ab:
            oab_ref[...] = _dot(hb, wab_ref[...])
        if with_h:
            oh_ref[...] = h.T.astype(BF16)

    o_ref[...] = _dot(h_ref[...], w_ref[...])


def _norm_proj(x2d, g, mods, shift_chunk, scale_chunk, rows_per_mod, mod_row0, w_bf16, w_ab=None,
               with_h=False, cols=None, tn=1024, name="norm_proj"):
    rows, d = x2d.shape
    col0, n = cols if cols is not None else (0, w_bf16.shape[1])
    with_ab = w_ab is not None
    tm = min(1024, rows_per_mod)
    assert rows % tm == 0 and rows_per_mod % tm == 0 and n % tn == 0 and col0 % tn == 0
    jb0 = col0 // tn

    def batch_of_tile(i):
        return mod_row0 + (i * tm) // rows_per_mod

    in_specs = [pl.BlockSpec((tm, d), lambda i, j: (i, 0)),
                pl.BlockSpec((1, d), lambda i, j: (0, 0)),
                pl.BlockSpec((1, 1, d), lambda i, j: (batch_of_tile(i), 0, shift_chunk)),
                pl.BlockSpec((1, 1, d), lambda i, j: (batch_of_tile(i), 0, scale_chunk)),
                pl.BlockSpec((d, tn), lambda i, j: (0, jb0 + j))]
    args = [x2d, g.reshape(1, d), mods, mods, w_bf16]
    out_specs = [pl.BlockSpec((tm, tn), lambda i, j: (i, j))]
    out_shape = [jax.ShapeDtypeStruct((rows, n), F32)]
    if with_ab:
        in_specs.append(pl.BlockSpec((d, LANES), lambda i, j: (0, 0)))
        args.append(w_ab)
        out_specs.append(pl.BlockSpec((tm, LANES), lambda i, j: (i, 0)))
        out_shape.append(jax.ShapeDtypeStruct((rows, LANES), F32))
    if with_h:
        out_specs.append(pl.BlockSpec((d, tm), lambda i, j: (0, i)))
        out_shape.append(jax.ShapeDtypeStruct((d, rows), BF16))
    return pl.pallas_call(
        functools.partial(_norm_proj_kernel, with_ab=with_ab, with_h=with_h),
        grid=(rows // tm, n // tn),
        in_specs=in_specs,
        out_specs=out_specs,
        out_shape=out_shape,
        scratch_shapes=[pltpu.VMEM((tm, d), BF16)],
        compiler_params=_params(("arbitrary", "arbitrary")),
        name=name,
    )(*args)


CONV_ROWS = 4
CONV_CW = 256
CONV_PAD = 16


def _conv_kernel(a_ref, gt_ref, cw_ref, cb_ref, lg_ref, lb_ref, w_ref, o_ref, pad_ref, y_ref,
                 shift_ref):
    c = a_ref.shape[1]
    ncc = c // CONV_CW
    u = a_ref[...] * _sigmoid(gt_ref[...])
    zero = jnp.zeros((CONV_PAD, CONV_CW), F32)
    for r in range(CONV_ROWS):
        for cc in range(ncc):
            pad_ref[r, cc, 0:CONV_PAD, :] = zero
            pad_ref[r, cc, CONV_PAD:CONV_PAD + GRID_W, :] = (
                u[r * GRID_W:(r + 1) * GRID_W, cc * CONV_CW:(cc + 1) * CONV_CW])
            pad_ref[r, cc, CONV_PAD + GRID_W:2 * CONV_PAD + GRID_W, :] = zero

    half = CONV_WIDTH // 2

    sub = 8
    span = GRID_W + 2 * CONV_PAD - sub

    def body(idx, carry):
        r = idx // ncc
        cc = idx % ncc
        for s in range(1, sub):
            shift_ref[s - 1] = pad_ref[r, cc, s:s + span, :]
        acc = jnp.zeros((GRID_W, CONV_CW), F32)
        for k in range(CONV_WIDTH):
            start = CONV_PAD - half + k
            s = start % sub
            base = start - s
            if s == 0:
                src = pad_ref[r, cc, base:base + GRID_W, :]
            else:
                src = shift_ref[s - 1, base:base + GRID_W, :]
            acc = acc + src * cw_ref[cc, k:k + 1, :]
        y_ref[cc, pl.ds(pl.multiple_of(r * GRID_W, GRID_W), GRID_W), :] = acc
        return carry

    lax.fori_loop(0, CONV_ROWS * ncc, body, 0)

    y = jnp.concatenate([y_ref[cc] for cc in range(ncc)], axis=-1) + cb_ref[...]
    mu = jnp.mean(y, axis=-1, keepdims=True)
    yc = y - mu
    var = jnp.mean(yc * yc, axis=-1, keepdims=True)
    yn = yc * lax.rsqrt(var + EPS) * lg_ref[...] + lb_ref[...]
    o_ref[...] = _dot(_silu(yn).astype(BF16), w_ref[...])


def _conformer(p, conv_w, conv_b, ln_g, ln_b, w_out_bf16, c):
    rows = p.shape[0]
    tm = CONV_ROWS * GRID_W
    ncc = c // CONV_CW
    cw = jnp.pad(conv_w, ((0, 32 - CONV_WIDTH), (0, 0)))
    cw = cw.reshape(32, ncc, CONV_CW).transpose(1, 0, 2)
    return pl.pallas_call(
        _conv_kernel,
        grid=(rows // tm,),
        in_specs=[pl.BlockSpec((tm, c), lambda i: (i, 0)),
                  pl.BlockSpec((tm, c), lambda i: (i, 1)),
                  pl.BlockSpec((ncc, 32, CONV_CW), lambda i: (0, 0, 0)),
                  pl.BlockSpec((1, c), lambda i: (0, 0)),
                  pl.BlockSpec((1, c), lambda i: (0, 0)),
                  pl.BlockSpec((1, c), lambda i: (0, 0)),
                  pl.BlockSpec((c, c), lambda i: (0, 0))],
        out_specs=pl.BlockSpec((tm, c), lambda i: (i, 0)),
        out_shape=jax.ShapeDtypeStruct((rows, c), F32),
        scratch_shapes=[pltpu.VMEM((CONV_ROWS, ncc, GRID_W + 2 * CONV_PAD, CONV_CW), F32),
                        pltpu.VMEM((ncc, tm, CONV_CW), F32),
                        pltpu.VMEM((7, GRID_W + 2 * CONV_PAD - 8, CONV_CW), F32)],
        compiler_params=_params(("arbitrary",)),
        name="conformer",
    )(p, p, cw, conv_b.reshape(1, c), ln_g.reshape(1, c), ln_b.reshape(1, c), w_out_bf16)


DN_PAD = 8
DN_RC = 256


def _dnprep_kernel(q_ref, k_ref, v_ref, wq_ref, wk_ref, wv_ref, qo_ref, ko_ref, vo_ref, pad_ref):
    length = q_ref.shape[0]
    zero = jnp.zeros((DN_PAD, LANES), F32)
    half = SHORT_CONV // 2

    def run(x_ref, w_ref, o_ref, mode):
        pad_ref[0:DN_PAD, :] = zero
        pad_ref[DN_PAD + length:2 * DN_PAD + length, :] = zero
        pad_ref[DN_PAD:DN_PAD + length, :] = x_ref[...]
        for r0 in range(0, length, DN_RC):
            acc = jnp.zeros((DN_RC, LANES), F32)
            for k in range(SHORT_CONV):
                start = DN_PAD - half + k + r0
                acc = acc + pad_ref[start:start + DN_RC, :] * w_ref[k:k + 1, :]
            y = _silu(acc)
            if mode != "v":
                y = y * lax.rsqrt(jnp.sum(y * y, axis=-1, keepdims=True) + EPS)
            if mode == "q":
                y = y * (DN_DK ** -0.5)
            o_ref[0, r0:r0 + DN_RC, :] = y.astype(o_ref.dtype)

    run(q_ref, wq_ref, qo_ref, "q")
    run(k_ref, wk_ref, ko_ref, "k")
    run(v_ref, wv_ref, vo_ref, "v")


def _dn_prep(p, col0, short_w8, batch, length):
    nh = DN_HEADS
    out = jax.ShapeDtypeStruct((batch, length, nh * LANES), BF16)
    ospec = pl.BlockSpec((1, length, LANES), lambda b, h: (b, 0, h))
    return pl.pallas_call(
        _dnprep_kernel,
        grid=(batch, nh),
        in_specs=[pl.BlockSpec((length, LANES), lambda b, h: (b, col0 + h)),
                  pl.BlockSpec((length, LANES), lambda b, h: (b, col0 + nh + h)),
                  pl.BlockSpec((length, LANES), lambda b, h: (b, col0 + 2 * nh + h)),
                  pl.BlockSpec((8, LANES), lambda b, h: (0, h)),
                  pl.BlockSpec((8, LANES), lambda b, h: (0, nh + h)),
                  pl.BlockSpec((8, LANES), lambda b, h: (0, 2 * nh + h))],
        out_specs=[ospec, ospec, ospec],
        out_shape=[out, out, out],
        scratch_shapes=[pltpu.VMEM((length + 2 * DN_PAD, LANES), F32)],
        compiler_params=_params(("arbitrary", "arbitrary")),
        name="dn_prep",
    )(p, p, p, short_w8, short_w8, short_w8)


DN_HB = 16
DN_SCAN_VMEM = 58 << 20
DN_BASE = 8


def _scan_kernel(qc_ref, kc_ref, vc_ref, abc_ref, ql_ref, kl_ref, vl_ref, abl_ref,
                 alog_ref, dtb_ref, o_ref, s_ref):
    ch = DN_CHUNK
    hb = DN_HB
    hg = pl.program_id(1)
    d = pl.program_id(2)
    fwd = d == 0

    row = lax.broadcasted_iota(jnp.int32, (ch, ch), 0)
    col = lax.broadcasted_iota(jnp.int32, (ch, ch), 1)
    diff = jnp.where(fwd, row - col, col - row)
    incl = diff >= 0
    strict = diff > 0
    tri = incl.astype(F32)

    def same_block(size):
        shift = size.bit_length() - 1
        return (row >> shift) == (col >> shift)

    base_mask = same_block(DN_BASE) & strict
    level_masks = []
    size = DN_BASE
    while size < ch:
        level_masks.append(same_block(2 * size) & jnp.logical_not(same_block(size)) & strict)
        size *= 2

    lane0 = 2 * DN_HEADS * d + hg * hb

    neg_a = -jnp.exp(alog_ref[...])
    dtb = dtb_ref[...]

    s_ref[...] = jnp.zeros_like(s_ref)

    heads = range(hb)

    def each(fn, *lists):
        return [fn(*args) for args in zip(*lists)]

    def inverse_minus_identity(ms):
        m0 = each(lambda m: jnp.where(base_mask, m, 0.0), ms)
        m016 = each(lambda t: t.astype(BF16), m0)
        a = each(lambda t: _dot(t, t), m016)
        a16 = each(lambda t: t.astype(BF16), a)
        b = each(lambda t: _dot(t, t), a16)
        f = each(lambda a_, m0_, m16, a16_: a_ - m0_ - _dot(m16, a16_), a, m0, m016, a16)
        e = each(lambda f_, b_: f_ + b_ + _dot(f_.astype(BF16), b_.astype(BF16)), f, b)
        for mask in level_masks:
            c = each(lambda m: jnp.where(mask, m, 0.0), ms)
            tc = each(lambda c_, e_: c_ + _dot(e_.astype(BF16), c_.astype(BF16)), c, e)
            e = each(lambda e_, tc_: e_ - tc_ - _dot(tc_.astype(BF16), e_.astype(BF16)), e, tc)
        return e

    def phase(q_ref, k_ref, v_ref, ab_ref, n_chunks, write_out):
        def body(p, carry):
            ci = jnp.where(fwd, p, n_chunks - 1 - p)
            r0 = pl.multiple_of(ci * ch, ch)
            ab = ab_ref[0, pl.ds(r0, ch), :]
            x = ab + dtb
            softplus = jnp.maximum(x, 0.0) + jnp.log(1.0 + jnp.exp(-jnp.abs(x)))
            g_all = neg_a * softplus
            b_all = _sigmoid(ab)
            gcum_all = _dot(tri, g_all, HI)
            gcum_t = gcum_all.T
            gtot_all = jnp.sum(g_all, axis=0, keepdims=True)
            q16 = [q_ref[0, pl.ds(r0, ch), c * ch:(c + 1) * ch] for c in heads]
            k16 = [k_ref[0, pl.ds(r0, ch), c * ch:(c + 1) * ch] for c in heads]
            q = each(lambda t: t.astype(F32), q16)
            k = each(lambda t: t.astype(F32), k16)
            v = [v_ref[0, pl.ds(r0, ch), c * ch:(c + 1) * ch].astype(F32) for c in heads]
            s = [s_ref[c] for c in heads]
            la = [lane0 + c for c in heads]
            gc_col = [jnp.sum(jnp.where(col == l, gcum_all, 0.0), axis=1, keepdims=True)
                      for l in la]
            beta_col = [jnp.sum(jnp.where(col == l + DN_HEADS, b_all, 0.0), axis=1, keepdims=True)
                        for l in la]
            gc_row = [jnp.sum(jnp.where(row == l, gcum_t, 0.0), axis=0, keepdims=True) for l in la]
            gt = [jnp.sum(jnp.where(col[0:1] == l, gtot_all, 0.0), axis=1, keepdims=True)
                  for l in la]
            decay = each(lambda gc, gr: jnp.where(incl, jnp.exp(gc - gr), 0.0), gc_col, gc_row)
            eg = each(jnp.exp, gc_col)
            if write_out:
                qk = each(lambda q_, k_: _dot_nt(jnp.concatenate([q_, k_], axis=0), k_), q16, k16)
                attn = each(lambda t, dc: t[:ch] * dc, qk, decay)
                kk = each(lambda t: t[ch:], qk)
            else:
                kk = each(lambda k_: _dot_nt(k_, k_), k16)
            m = each(lambda kk_, b_, dc: jnp.where(strict, kk_ * b_ * dc, 0.0), kk, beta_col, decay)
            e = inverse_minus_identity(m)
            rhs = each(lambda v_, k_, b_, eg_: jnp.concatenate([v_ * b_, k_ * (b_ * eg_)], axis=1),
                       v, k, beta_col, eg)
            uw = each(lambda r, e_: r + _dot(e_.astype(BF16), r.astype(BF16)), rhs, e)
            u = each(lambda t: t[:, :LANES], uw)
            w = each(lambda t: t[:, LANES:], uw)
            s16 = each(lambda t: t.astype(BF16), s)
            kdt = each(lambda k_, gt_, gr: k_.T * jnp.exp(gt_ - gr), k, gt, gc_row)
            if write_out:
                ws = each(lambda w_, q_, eg_, s_: _dot(
                    jnp.concatenate([w_, q_ * eg_], axis=0).astype(BF16), s_), w, q, eg, s16)
                v_new = each(lambda u_, ws_: u_ - ws_[:ch], u, ws)
                r2 = each(lambda a_, kd, vn: _dot(
                    jnp.concatenate([a_, kd], axis=0).astype(BF16), vn.astype(BF16)),
                    attn, kdt, v_new)
                o = each(lambda ws_, r2_: ws_[ch:] + r2_[:ch], ws, r2)
                ds = each(lambda t: t[ch:], r2)
            else:
                v_new = each(lambda u_, w_, s_: u_ - _dot(w_.astype(BF16), s_), u, w, s16)
                ds = each(lambda kd, vn: _dot(kd.astype(BF16), vn.astype(BF16)), kdt, v_new)
            s_new = each(lambda s_, gt_, ds_: s_ * jnp.exp(gt_) + ds_, s, gt, ds)
            for c in heads:
                if write_out:
                    o_ref[0, 0, pl.ds(r0, ch), c * ch:(c + 1) * ch] = o[c].astype(o_ref.dtype)
                s_ref[c] = s_new[c]
            return carry

        lax.fori_loop(0, n_chunks, body, 0)

    phase(qc_ref, kc_ref, vc_ref, abc_ref, qc_ref.shape[1] // ch, False)
    phase(ql_ref, kl_ref, vl_ref, abl_ref, ql_ref.shape[1] // ch, True)


def _dn_scan(qc, kc, vc, abc, ql, kl, vl, abl, alog_l, dtb_l):
    batch, lc, hv = qc.shape
    ll = ql.shape[1]
    hb = DN_HB
    w = hb * LANES
    once = pl.Buffered(1)
    cspec = pl.BlockSpec((1, lc, w), lambda b, g, d: (b, 0, g), pipeline_mode=once)
    lspec = pl.BlockSpec((1, ll, w), lambda b, g, d: (b, 0, g), pipeline_mode=once)
    return pl.pallas_call(
        _scan_kernel,
        grid=(batch, DN_HEADS // hb, 2),
        in_specs=[cspec, cspec, cspec,
                  pl.BlockSpec((1, lc, LANES), lambda b, g, d: (b, 0, 0)),
                  lspec, lspec, lspec,
                  pl.BlockSpec((1, ll, LANES), lambda b, g, d: (b, 0, 0)),
                  pl.BlockSpec((1, LANES), lambda b, g, d: (0, 0)),
                  pl.BlockSpec((1, LANES), lambda b, g, d: (0, 0))],
        out_specs=pl.BlockSpec((1, 1, ll, w), lambda b, g, d: (d, b, 0, g)),
        out_shape=jax.ShapeDtypeStruct((2, batch, ll, hv), BF16),
        scratch_shapes=[pltpu.VMEM((hb, DN_DK, LANES), F32)],
        compiler_params=_params(("arbitrary", "arbitrary", "arbitrary"), DN_SCAN_VMEM),
        name="dn_scan",
    )(qc, kc, vc, abc, ql, kl, vl, abl, alog_l, dtb_l)


def _merge_kernel(yc_ref, of_ref, ob_ref, z_ref, brc_ref, brd_ref, g_ref, wdn_ref, wout_ref,
                  x_ref, m2_ref, out_ref):
    o = of_ref[0].astype(F32) + ob_ref[0].astype(F32)
    parts = []
    for h in range(DN_HEADS):
        oh = o[:, h * LANES:(h + 1) * LANES]
        ms = jnp.mean(oh * oh, axis=-1, keepdims=True)
        parts.append(oh * lax.rsqrt(ms + EPS))
    on = jnp.concatenate(parts, axis=-1) * g_ref[...]
    gated = on * _silu(z_ref[...])
    y_dn = _dot(gated.astype(BF16), wdn_ref[...])
    mix = _sigmoid(brc_ref[...]) * yc_ref[...] + _sigmoid(brd_ref[...]) * y_dn
    y = _dot(mix.astype(BF16), wout_ref[...])
    out_ref[...] = x_ref[...] + m2_ref[0] * y


def _merge(y_conv, o2, p, zcol, brcol, norm_g_l, wdn, wout, x2d, mods, seq, tm=256):
    rows, d = x2d.shape
    rspec = lambda cb: pl.BlockSpec((tm, d), lambda i: (i, cb))
    ospec = lambda dr: pl.BlockSpec((1, tm, d), lambda i: (dr, i, 0))
    wspec = pl.BlockSpec((d, d), lambda i: (0, 0), pipeline_mode=pl.Buffered(1))
    return pl.pallas_call(
        _merge_kernel,
        grid=(rows // tm,),
        in_specs=[rspec(0), ospec(0), ospec(1), rspec(zcol), rspec(brcol), rspec(brcol + 1),
                  pl.BlockSpec((1, d), lambda i: (0, 0)), wspec, wspec, rspec(0),
                  pl.BlockSpec((1, 1, d), lambda i: ((i * tm) // seq, 0, 2))],
        out_specs=rspec(0),
        out_shape=jax.ShapeDtypeStruct((rows, d), F32),
        compiler_params=_params(("arbitrary",)),
        name="merge",
    )(y_conv, o2, o2, p, p, p, norm_g_l, wdn, wout, x2d, mods)


ROUTE_TM = 256


def _top_values(vals, count):
    outs = []
    for _ in range(count):
        m = jnp.max(vals, axis=0, keepdims=True)
        outs.append(m)
        vals = jnp.where(vals == m, NEG_BIG, vals)
    return outs


def _route_kernel(q_ref, k1_ref, k2_ref, thr_ref, s2_ref, p1_ref, p2_ref):
    q = q_ref[...]
    half = PEER_KEYS
    n = PEER_TOPK + 1
    for h in range(PEER_HEADS):
        q1 = q[:, (2 * h) * half:(2 * h + 1) * half]
        q2 = q[:, (2 * h + 1) * half:(2 * h + 2) * half]
        s1 = _dot_nt(k1_ref[h], q1, HI)
        s2 = _dot_nt(k2_ref[h], q2, HI)
        v1 = _top_values(s1, n)
        v2 = _top_values(s2, n)
        v2s = jnp.concatenate(v2, axis=0)
        counts = [n // (a + 1) for a in range(n)]
        pad = -sum(counts) % 8
        tokens = v2s.shape[1]
        second = jnp.concatenate([v2s[0:c] for c in counts]
                                 + [jnp.full((pad, tokens), NEG_BIG, F32)], axis=0)
        cand = jnp.concatenate([v1[a] + v2s[0:c] for a, c in enumerate(counts)]
                               + [jnp.full((pad, tokens), NEG_BIG, F32)], axis=0)
        tops = _top_values(cand, n)
        cut = 0.5 * (tops[PEER_TOPK - 1] + tops[PEER_TOPK])
        need = jnp.concatenate([jnp.broadcast_to(cut - v1[a], (c, tokens))
                                for a, c in enumerate(counts)]
                               + [jnp.full((pad, tokens), -NEG_BIG, F32)], axis=0)
        best = v1[0] + v2[0]
        zsum = jnp.sum(jnp.where(second >= need, jnp.exp(cand - best), 0.0), axis=0,
                       keepdims=True)
        thr_ref[h] = cut - s1
        s2_ref[h] = s2
        p1_ref[h] = jnp.exp(s1 - v1[0]) / zsum
        p2_ref[h] = jnp.exp(s2 - v2[0])


def _route(q2d, key1, key2):
    rows, d = q2d.shape
    tm = ROUTE_TM
    big = jax.ShapeDtypeStruct((PEER_HEADS, PEER_KEYS, rows), F32)
    bspec = pl.BlockSpec((PEER_HEADS, PEER_KEYS, tm), lambda i: (0, 0, i))
    kspec = pl.BlockSpec((PEER_HEADS, PEER_KEYS, PEER_KEYS), lambda i: (0, 0, 0))
    return pl.pallas_call(
        _route_kernel,
        grid=(rows // tm,),
        in_specs=[pl.BlockSpec((tm, d), lambda i: (i, 0)), kspec, kspec],
        out_specs=[bspec, bspec, bspec, bspec],
        out_shape=[big, big, big, big],
        compiler_params=_params(("arbitrary",)),
        name="peer_route",
    )(q2d, key1, key2)


PEER_TM = 512
PEER_TE = 1024
PEER_WC = 256
GELU_C = 0.7978845608028654


def _experts_kernel(ht_ref, u_ref, vt_ref, thr_ref, s2_ref, p1_ref, p2_ref, o_ref, acc_ref, g_ref):
    e = pl.program_id(1)
    na = PEER_TE // PEER_KEYS

    @pl.when(e == 0)
    def _():
        acc_ref[...] = jnp.zeros_like(acc_ref)

    a_t = _dot(u_ref[...], ht_ref[...])
    for al in range(na):
        rows = slice(al * PEER_KEYS, (al + 1) * PEER_KEYS)
        for tc in range(PEER_TM // PEER_WC):
            ts = slice(tc * PEER_WC, (tc + 1) * PEER_WC)
            wd = jnp.zeros((PEER_KEYS, PEER_WC), F32)
            for h in range(PEER_HEADS):
                need = thr_ref[h, al:al + 1, ts]
                p1r = p1_ref[h, al:al + 1, ts]
                wd = wd + jnp.where(s2_ref[h, :, ts] >= need, p1r * p2_ref[h, :, ts], 0.0)
            x = a_t[rows, ts]
            inner = x * (GELU_C + (GELU_C * 0.044715) * (x * x))
            hx = 0.5 * x
            g_ref[rows, ts] = ((hx + hx * jnp.tanh(inner)) * wd).astype(BF16)
    acc_ref[...] += _dot(vt_ref[...], g_ref[...])

    @pl.when(e == pl.num_programs(1) - 1)
    def _():
        o_ref[...] = acc_ref[...]


def _experts(h_t, u_bf16, vt_bf16, thr, s2, p1, p2):
    d, rows = h_t.shape
    ne = u_bf16.shape[0]
    tm, te = PEER_TM, PEER_TE
    na = te // PEER_KEYS
    assert na == 8 and ne % te == 0
    bspec = pl.BlockSpec((PEER_HEADS, PEER_KEYS, tm), lambda i, e: (0, 0, i))
    aspec = pl.BlockSpec((PEER_HEADS, na, tm), lambda i, e: (0, e, i))
    return pl.pallas_call(
        _experts_kernel,
        grid=(rows // tm, ne // te),
        in_specs=[pl.BlockSpec((d, tm), lambda i, e: (0, i)),
                  pl.BlockSpec((te, d), lambda i, e: (e, 0)),
                  pl.BlockSpec((d, te), lambda i, e: (0, e)),
                  aspec, bspec, aspec, bspec],
        out_specs=pl.BlockSpec((d, tm), lambda i, e: (0, i)),
        out_shape=jax.ShapeDtypeStruct((d, rows), F32),
        scratch_shapes=[pltpu.VMEM((d, tm), F32), pltpu.VMEM((te, tm), BF16)],
        compiler_params=_params(("arbitrary", "arbitrary")),
        name="peer_experts",
    )(h_t, u_bf16, vt_bf16, thr, s2, p1, p2)


def _tables_kernel(u_ref, v_ref, u16_ref, vt16_ref):
    u16_ref[...] = u_ref[...].astype(BF16)
    vt16_ref[...] = v_ref[...].T.astype(BF16)


def _expert_tables(peer_u, peer_v):
    ne, d = peer_u.shape
    te = 512
    return pl.pallas_call(
        _tables_kernel,
        grid=(ne // te,),
        in_specs=[pl.BlockSpec((te, d), lambda e: (e, 0)), pl.BlockSpec((te, d), lambda e: (e, 0))],
        out_specs=[pl.BlockSpec((te, d), lambda e: (e, 0)), pl.BlockSpec((d, te), lambda e: (0, e))],
        out_shape=[jax.ShapeDtypeStruct((ne, d), BF16), jax.ShapeDtypeStruct((d, ne), BF16)],
        compiler_params=_params(("arbitrary",)),
        name="peer_tables",
    )(peer_u, peer_v)


def _final_kernel(x_ref, yt_ref, m5_ref, g_ref, o_ref):
    x = x_ref[...] + m5_ref[0] * yt_ref[...].T
    ms = jnp.mean(x * x, axis=-1, keepdims=True)
    o_ref[...] = x * lax.rsqrt(ms + EPS) * g_ref[...]


def _final(x1, y_t, mods, final_g, seq, tm=512):
    rows, d = x1.shape
    rspec = pl.BlockSpec((tm, d), lambda i: (i, 0))
    return pl.pallas_call(
        _final_kernel,
        grid=(rows // tm,),
        in_specs=[rspec, pl.BlockSpec((d, tm), lambda i: (0, i)),
                  pl.BlockSpec((1, 1, d), lambda i: ((i * tm) // seq, 0, 5)),
                  pl.BlockSpec((1, d), lambda i: (0, 0))],
        out_specs=rspec,
        out_shape=jax.ShapeDtypeStruct((rows, d), F32),
        compiler_params=_params(("arbitrary",)),
        name="final",
    )(x1, y_t, mods, final_g.reshape(1, d))


def kernel(x, c, ctx, c_ctx, w_mod, b_mod, norm1_g, norm2_g, w_in, conv_w, conv_b, conv_ln_g,
           conv_ln_b, w_conv_out, dn_short_w, dn_a_log, dn_dt_bias, dn_norm_g, w_dn_out, w_out,
           peer_w_q, peer_key1, peer_key2, peer_u, peer_v, final_g):
    batch, seq, d = x.shape
    lc = ctx.shape[1]
    assert w_mod.shape[0] == 1, "single-layer block"
    cdim = conv_w.shape[2]
    nqk = DN_HEADS * DN_DK
    col_glu = 2 * cdim
    col_qkv = 3 * nqk
    col_ab = 4 * DN_HEADS
    s_qkv = col_glu
    s_z = s_qkv + col_qkv
    s_ab = s_z + nqk
    s_br = s_ab + col_ab

    c8 = jnp.zeros((8, d), F32).at[:batch].set(c).at[batch].set(c_ctx)
    mods = _mod(c8, w_mod[0], b_mod[0]).reshape(8, 1, N_MOD * d)

    w_main, w_ab = _in_weights(w_in[0], s_ab, s_br)

    x2d = x.reshape(batch * seq, d)
    p, ab_l = _norm_proj(x2d, norm1_g[0], mods, 0, 1, seq, 0, w_main, w_ab=w_ab, name="in_proj")
    pc, ab_c = _norm_proj(ctx.reshape(batch * lc, d), norm1_g[0], mods, 0, 1, batch * lc, batch,
                          w_main, w_ab=w_ab, cols=(s_qkv, col_qkv), name="in_proj_ctx")

    y_conv = _conformer(p, conv_w[0], conv_b[0], conv_ln_g[0], conv_ln_b[0],
                        w_conv_out[0].astype(BF16), cdim)

    short_w8 = jnp.pad(dn_short_w[0], ((0, 8 - SHORT_CONV), (0, 0)))
    ql, kl, vl = _dn_prep(p, s_qkv // LANES, short_w8, batch, seq)
    qc, kc, vc = _dn_prep(pc, 0, short_w8, batch, lc)
    gate_lanes = lambda t: jnp.pad(
        jnp.pad(t, ((0, 0), (0, DN_HEADS))).reshape(1, col_ab), ((0, 0), (0, LANES - col_ab)))
    o_l = _dn_scan(qc, kc, vc, ab_c.reshape(batch, lc, LANES), ql, kl, vl,
                   ab_l.reshape(batch, seq, LANES), gate_lanes(dn_a_log[0]),
                   gate_lanes(dn_dt_bias[0]))

    zcol = s_z // d
    brcol = (s_br - col_ab) // d
    x1 = _merge(y_conv, o_l.reshape(2, batch * seq, nqk), p, zcol, brcol,
                jnp.tile(dn_norm_g[0], DN_HEADS).reshape(1, nqk), w_dn_out[0].astype(BF16),
                w_out[0].astype(BF16), x2d, mods, seq)

    q2d, h2_t = _norm_proj(x1, norm2_g[0], mods, 3, 4, seq, 0, peer_w_q[0].astype(BF16),
                           with_h=True, name="peer_q")
    thr, s2, p1, p2 = _route(q2d, peer_key1[0], peer_key2[0])
    u16, vt16 = _expert_tables(peer_u[0], peer_v[0])
    y_t = _experts(h2_t, u16, vt16, thr, s2, p1, p2)
    out = _final(x1, y_t, mods, final_g, seq)
    return out.reshape(batch, seq, d)
```
